```python
import math
import jax
import jax.numpy as jnp
from jax import lax
import numpy as np

D_MODEL = 1024
BATCH = 8
SEQ = 4096
DEPTH = 1

CTX_LEN = 256
GRID_W = 64
NORM_EPS = 1e-6

RW_HEAD = 64
RW_HEADS = D_MODEL // RW_HEAD
RW_DIM = RW_HEADS * RW_HEAD
RW_DECAY_LORA = 64
RW_AAA_LORA = 64
RW_GATE_LORA = 128
RW_GN_EPS = 64e-5
RW_COLS = 3 * RW_DIM + 2 * RW_DECAY_LORA + 2 * RW_AAA_LORA + RW_GATE_LORA
RW_SPLITS = (RW_DIM, 2 * RW_DIM, 3 * RW_DIM, 3 * RW_DIM + 2 * RW_DECAY_LORA, 3 * RW_DIM + 2 * RW_DECAY_LORA + 2 * RW_AAA_LORA)

SSM_DIM = 2 * D_MODEL
SSM_HEAD = 64
SSM_HEADS = SSM_DIM // SSM_HEAD
SSM_GROUPS = 4
SSM_HPG = SSM_HEADS // SSM_GROUPS
SSM_STATE = 128
SSM_CONV = 5
SSM_CHUNK = 128
SSM_BC = SSM_GROUPS * SSM_STATE
SSM_CONV_DIM = SSM_DIM + 2 * SSM_BC
SSM_COLS = SSM_DIM + SSM_CONV_DIM + 2 * SSM_HEADS

GATE_COLS = 2 * D_MODEL
N_IN = RW_COLS + SSM_COLS + GATE_COLS
IN_SPLITS = (RW_COLS, RW_COLS + SSM_COLS)

N_EXPERTS = 16
EXPERT_FF = 1024
EC_CAPACITY = 2

F32 = jnp.float32

kernel_name = 'hybrid_rwkv7_mamba2_ec_moe_dit_block'


def rms_norm(x, g):
    xf = x.astype(F32)
    y = xf * lax.rsqrt(jnp.mean(xf * xf, axis=-1, keepdims=True) + NORM_EPS)
    return (y * g.astype(F32)).astype(x.dtype)


def adaln_mod(cond, w, b):
    m = jnp.einsum('...d,de->...e', jax.nn.silu(cond), w) + b
    return jnp.split(m[..., None, :], 6, axis=-1)


def centred_shift(u, mu):
    zero = jnp.zeros_like(u[:, :1])
    prev = jnp.concatenate([zero, u[:, :-1]], axis=1)
    nxt = jnp.concatenate([u[:, 1:], zero], axis=1)
    return u + mu[0] * (prev - u) + mu[1] * (nxt - u)


def dw_conv_centred(u, w, b):
    pad = w.shape[1] // 2
    out = lax.conv_general_dilated(u, jnp.transpose(w)[:, None, :].astype(u.dtype), window_strides=(1,), padding=[(pad, pad)], dimension_numbers=('NWC', 'WIO', 'NWC'), feature_group_count=u.shape[-1])
    return out + b


def to_colmajor(t, rows):
    b, n, ch = t.shape
    return t.reshape(b, rows, GRID_W, ch).swapaxes(1, 2).reshape(b, n, ch)


def from_colmajor(t, rows):
    b, n, ch = t.shape
    return t.reshape(b, GRID_W, rows, ch).swapaxes(1, 2).reshape(b, n, ch)


def rwkv_prep(u, p):
    b_, n = u.shape[:2]
    r, k, v, xw, xa, xg = jnp.split(u, RW_SPLITS, axis=-1)
    xw = xw.reshape(b_, n, 2, RW_DECAY_LORA)
    xa = xa.reshape(b_, n, 2, RW_AAA_LORA)
    w_raw = (p['rw_w0'] + jnp.einsum('blde,def->bldf', jnp.tanh(xw), p['rw_w2'])).astype(F32)
    decay = jnp.exp(-jnp.exp(-jax.nn.softplus(-w_raw) - 0.5))
    a = jax.nn.sigmoid(p['rw_a0'] + jnp.einsum('blde,def->bldf', xa, p['rw_a2']))
    g = jnp.einsum('ble,ef->blf', jax.nn.sigmoid(xg), p['rw_g2'])
    kk = (k * p['rw_kk']).astype(F32).reshape(b_, n, RW_HEADS, RW_HEAD)
    kk = kk * lax.rsqrt(jnp.maximum(jnp.sum(kk * kk, axis=-1, keepdims=True), 1e-12))
    k_eff = k[:, :, None] * (1 + (a - 1) * p['rw_ka'])
    hd = lambda t: t.reshape(*t.shape[:-1], RW_HEADS, RW_HEAD)
    return dict(r=hd(r), w=hd(decay), k=hd(k_eff), v=hd(v), kk=kk, a=hd(a), g=g)


def rwkv_scan(q, d, s0, reverse, with_output):
    seq = (q['r'], q['w'][:, :, d], q['k'][:, :, d], q['v'], q['kk'], q['a'][:, :, d])
    seq = tuple(jnp.moveaxis(t.astype(F32), 1, 0) for t in seq)

    def step(s, inp):
        r_t, w_t, k_t, v_t, kk_t, a_t = inp
        sa = jnp.einsum('bhvk,bhk->bhv', s, -kk_t)
        s = s * w_t[:, :, None, :] + sa[..., None] * (kk_t * a_t)[:, :, None, :] + v_t[..., None] * k_t[:, :, None, :]
        return s, (jnp.einsum('bhvk,bhk->bhv', s, r_t) if with_output else None)

    s, ys = lax.scan(step, s0, seq, reverse=reverse)
    return s, (jnp.moveaxis(ys, 0, 1) if with_output else None)


def rwkv_readout(y, q, p):
    b_, n = y.shape[:2]
    mean = jnp.mean(y, axis=-1, keepdims=True)
    var = jnp.mean(jnp.square(y - mean), axis=-1, keepdims=True)
    yn = ((y - mean) * lax.rsqrt(var + RW_GN_EPS)).reshape(b_, n, RW_DIM) * p['rw_ln_w'].astype(F32) + p['rw_ln_b'].astype(F32)
    bonus = jnp.einsum('blhn,bldhn,hn->blh', q['r'].astype(F32), q['k'].astype(F32), p['rw_rk'].astype(F32))[..., None] * q['v'].astype(F32)
    return (yn + bonus.reshape(b_, n, RW_DIM)) * q['g'].astype(F32)


def rwkv_branch(uc, ul, p, ctx_out):
    qc = rwkv_prep(centred_shift(uc, p['rw_mu']), p)
    ql = rwkv_prep(centred_shift(ul, p['rw_mu']), p)
    s0 = jnp.zeros((ul.shape[0], RW_HEADS, RW_HEAD, RW_HEAD), F32)
    sc_f, yc_f = rwkv_scan(qc, 0, s0, False, ctx_out)
    sc_b, yc_b = rwkv_scan(qc, 1, s0, True, ctx_out)
    _, yl_f = rwkv_scan(ql, 0, sc_f, False, True)
    _, yl_b = rwkv_scan(ql, 1, sc_b, True, True)
    out_l = rwkv_readout(yl_f + yl_b, ql, p)
    out_c = rwkv_readout(yc_f + yc_b, qc, p) if ctx_out else None
    return out_c, out_l


def ssm_prep(u, p):
    b_, n = u.shape[:2]
    z, xbc, dt = jnp.split(u, [SSM_DIM, SSM_DIM + SSM_CONV_DIM], axis=-1)
    xbc = jax.nn.silu(dw_conv_centred(xbc, p['ssm_conv_w'], p['ssm_conv_b']))
    xs, bm, cm = jnp.split(xbc, [SSM_DIM, SSM_DIM + SSM_BC], axis=-1)
    dt = jax.nn.softplus((dt.reshape(b_, n, 2, SSM_HEADS) + p['ssm_dt_bias']).astype(F32))
    return dict(z=z, x=xs.reshape(b_, n, SSM_HEADS, SSM_HEAD), b=bm.reshape(b_, n, SSM_GROUPS, SSM_STATE), c=cm.reshape(b_, n, SSM_GROUPS, SSM_STATE), dt=dt)


def ssd_chunked(xs, dt, bm, cm, a_head, h0, with_output):
    b_, n = xs.shape[:2]
    nc = n // SSM_CHUNK

    def chunks(t):
        t = t.astype(F32).reshape(b_, nc, SSM_CHUNK, *t.shape[2:])
        return jnp.moveaxis(t, 1, 0)

    xg = chunks(xs.reshape(b_, n, SSM_GROUPS, SSM_HPG, SSM_HEAD))
    dtg = chunks(dt.reshape(b_, n, SSM_GROUPS, SSM_HPG))
    bg, cg = chunks(bm), chunks(cm)
    ag = a_head.reshape(SSM_GROUPS, SSM_HPG)
    lower = jnp.tril(jnp.ones((SSM_CHUNK, SSM_CHUNK), bool))[None, :, :, None, None]

    def step(h, inp):
        x_c, dt_c, b_c, c_c = inp
        cum = jnp.cumsum(dt_c * ag, axis=1)
        xdt = x_c * dt_c[..., None]
        h_new = h * jnp.exp(cum[:, -1])[..., None, None] + jnp.einsum('bjgn,bjge,bjgep->bgepn', b_c, jnp.exp(cum[:, -1:] - cum), xdt)
        if not with_output:
            return h_new, None
        seg = jnp.exp(jnp.where(lower, cum[:, :, None] - cum[:, None, :], -jnp.inf))
        cb = jnp.einsum('bign,bjgn->bijg', c_c, b_c)
        y_intra = jnp.einsum('bijg,bijge,bjgep->bigep', cb, seg, xdt)
        y_inter = jnp.einsum('bign,bgepn->bigep', c_c, h) * jnp.exp(cum)[..., None]
        return h_new, y_intra + y_inter

    h, ys = lax.scan(step, h0, (xg, dtg, bg, cg))
    if not with_output:
        return h, None
    return h, jnp.moveaxis(ys, 0, 1).reshape(b_, n, SSM_HEADS, SSM_HEAD)


def ssd_direction(q, d, a_log, h0, reverse, with_output):
    seq = (q['x'], q['dt'][:, :, d], q['b'], q['c'])
    if reverse:
        seq = tuple(jnp.flip(t, axis=1) for t in seq)
    h, y = ssd_chunked(*seq, -jnp.exp(a_log[d].astype(F32)), h0, with_output)
    if reverse and with_output:
        y = jnp.flip(y, axis=1)
    return h, y


def ssm_readout(y, q, p):
    b_, n = y.shape[:2]
    y = (y + p['ssm_d'].astype(F32)[:, None] * q['x'].astype(F32)).reshape(b_, n, SSM_DIM)
    return rms_norm(y * jax.nn.silu(q['z'].astype(F32)), p['ssm_norm'])


def ssm_branch(uc, ul, p, rows, ctx_out):
    qc = ssm_prep(uc, p)
    ql = ssm_prep(to_colmajor(ul, rows), p)
    h0 = jnp.zeros((ul.shape[0], SSM_GROUPS, SSM_HPG, SSM_HEAD, SSM_STATE), F32)
    hc_f, yc_f = ssd_direction(qc, 0, p['ssm_a_log'], h0, False, ctx_out)
    hc_b, yc_b = ssd_direction(qc, 1, p['ssm_a_log'], h0, True, ctx_out)
    _, yl_f = ssd_direction(ql, 0, p['ssm_a_log'], hc_f, False, True)
    _, yl_b = ssd_direction(ql, 1, p['ssm_a_log'], hc_b, True, True)
    out_l = from_colmajor(ssm_readout(yl_f + yl_b, ql, p), rows)
    out_c = ssm_readout(yc_f + yc_b, qc, p) if ctx_out else None
    return out_c, out_l


def merge(ya, yb, gates, p):
    ga, gb = jnp.split(jax.nn.sigmoid(gates.astype(F32)), 2, axis=-1)
    ya = jnp.einsum('blc,cd->bld', ya, p['proj_a'])
    yb = jnp.einsum('blc,cd->bld', yb, p['proj_b'])
    return jnp.einsum('bld,de->ble', ga * ya + gb * yb, p['w_out']).astype(gates.dtype)


def mixer(hc, hl, p, rows, ctx_out):
    pc = jnp.einsum('bld,de->ble', hc, p['w_in'])
    pl = jnp.einsum('bld,de->ble', hl, p['w_in'])
    rw_c, ss_c, gt_c = jnp.split(pc, IN_SPLITS, axis=-1)
    rw_l, ss_l, gt_l = jnp.split(pl, IN_SPLITS, axis=-1)
    ya_c, ya_l = rwkv_branch(rw_c, rw_l, p, ctx_out)
    yb_c, yb_l = ssm_branch(ss_c, ss_l, p, rows, ctx_out)
    out_l = merge(ya_l, yb_l, gt_l, p)
    out_c = merge(ya_c, yb_c, gt_c, p) if ctx_out else None
    return out_c, out_l


def ec_moe(h, router, w1, w3, w2):
    b_, n, _ = h.shape
    cap = EC_CAPACITY * n // N_EXPERTS
    aff = jax.nn.softmax(jnp.einsum('bld,de->ble', h, router).astype(F32), axis=-1)
    gate, idx = lax.top_k(jnp.swapaxes(aff, 1, 2), cap)
    bidx = jnp.arange(b_)[:, None, None]
    xe = h[bidx, idx]
    hid = jax.nn.silu(jnp.einsum('becd,edf->becf', xe, w1)) * jnp.einsum('becd,edf->becf', xe, w3)
    ye = jnp.einsum('becf,efd->becd', hid, w2) * gate[..., None].astype(h.dtype)
    return jnp.zeros_like(h).at[bidx, idx].add(ye)


def setup_inputs(seed: int = 0) -> dict:
    key = jax.random.key(seed)
    ks = list(jax.random.split(key, 40))
    cnt = [0]

    def nxt():
        cnt[0] += 1
        return ks[cnt[0] - 1]

    def nrm(shape, scale):
        return scale * jax.random.normal(nxt(), shape, jnp.float32)

    def unif(shape, lo, hi):
        return jax.random.uniform(nxt(), shape, jnp.float32, lo, hi)

    L = DEPTH
    x = nrm((BATCH, SEQ, D_MODEL), 1.0)
    c = nrm((BATCH, D_MODEL), 1.0)
    ctx = nrm((BATCH, CTX_LEN, D_MODEL), 1.0)
    c_ctx = nrm((D_MODEL,), 1.0)
    ada_w = nrm((L, D_MODEL, 6 * D_MODEL), 0.5 * D_MODEL ** -0.5)
    ada_b = nrm((L, 6 * D_MODEL), 0.02)
    norm1_pre = 1.0 + nrm((L, D_MODEL), 0.02)
    norm1_post = 1.0 + nrm((L, D_MODEL), 0.02)
    norm2_pre = 1.0 + nrm((L, D_MODEL), 0.02)
    norm2_post = 1.0 + nrm((L, D_MODEL), 0.02)
    w_in = nrm((L, D_MODEL, N_IN), D_MODEL ** -0.5)
    rw_mu = unif((L, 2, RW_COLS), 0.0, 0.5)
    rw_w0 = unif((L, 2, RW_DIM), -6.0, -0.5)
    rw_w2 = nrm((L, 2, RW_DECAY_LORA, RW_DIM), 0.5 * RW_DECAY_LORA ** -0.5)
    rw_a0 = nrm((L, 2, RW_DIM), 0.1)
    rw_a2 = nrm((L, 2, RW_AAA_LORA, RW_DIM), 0.5 * RW_AAA_LORA ** -0.5)
    rw_g2 = nrm((L, RW_GATE_LORA, RW_DIM), RW_GATE_LORA ** -0.5)
    rw_kk = 0.85 + nrm((L, RW_DIM), 0.05)
    rw_ka = 1.0 + nrm((L, RW_DIM), 0.05)
    rw_rk = nrm((L, RW_HEADS, RW_HEAD), 0.1)
    rw_ln_w = 1.0 + nrm((L, RW_DIM), 0.02)
    rw_ln_b = nrm((L, RW_DIM), 0.02)
    ssm_conv_w = nrm((L, SSM_CONV_DIM, SSM_CONV), SSM_CONV ** -0.5)
    ssm_conv_b = nrm((L, SSM_CONV_DIM), 0.02)
    dt0 = jnp.exp(unif((L, 2, SSM_HEADS), math.log(1e-3), math.log(1e-1)))
    ssm_dt_bias = dt0 + jnp.log(-jnp.expm1(-dt0))
    ssm_a_log = jnp.log(unif((L, 2, SSM_HEADS), 1.0, 16.0))
    ssm_d = 1.0 + nrm((L, SSM_HEADS), 0.02)
    ssm_norm = 1.0 + nrm((L, SSM_DIM), 0.02)
    proj_a = nrm((L, RW_DIM, D_MODEL), RW_DIM ** -0.5)
    proj_b = nrm((L, SSM_DIM, D_MODEL), SSM_DIM ** -0.5)
    w_out = nrm((L, D_MODEL, D_MODEL), D_MODEL ** -0.5)
    router = nrm((L, D_MODEL, N_EXPERTS), D_MODEL ** -0.5)
    exp_w1 = nrm((L, N_EXPERTS, D_MODEL, EXPERT_FF), D_MODEL ** -0.5)
    exp_w3 = nrm((L, N_EXPERTS, D_MODEL, EXPERT_FF), D_MODEL ** -0.5)
    exp_w2 = nrm((L, N_EXPERTS, EXPERT_FF, D_MODEL), EXPERT_FF ** -0.5)
    return {'x': x, 'c': c, 'ctx': ctx, 'c_ctx': c_ctx, 'ada_w': ada_w, 'ada_b': ada_b,
            'norm1_pre': norm1_pre, 'norm1_post': norm1_post, 'norm2_pre': norm2_pre, 'norm2_post': norm2_post,
            'w_in': w_in, 'rw_mu': rw_mu, 'rw_w0': rw_w0, 'rw_w2': rw_w2, 'rw_a0': rw_a0, 'rw_a2': rw_a2,
            'rw_g2': rw_g2, 'rw_kk': rw_kk, 'rw_ka': rw_ka, 'rw_rk': rw_rk, 'rw_ln_w': rw_ln_w, 'rw_ln_b': rw_ln_b,
            'ssm_conv_w': ssm_conv_w, 'ssm_conv_b': ssm_conv_b, 'ssm_dt_bias': ssm_dt_bias, 'ssm_a_log': ssm_a_log,
            'ssm_d': ssm_d, 'ssm_norm': ssm_norm, 'proj_a': proj_a, 'proj_b': proj_b, 'w_out': w_out,
            'router': router, 'exp_w1': exp_w1, 'exp_w3': exp_w3, 'exp_w2': exp_w2}


def reference(x, c, ctx, c_ctx, ada_w, ada_b, norm1_pre, norm1_post, norm2_pre, norm2_post,
              w_in, rw_mu, rw_w0, rw_w2, rw_a0, rw_a2, rw_g2, rw_kk, rw_ka, rw_rk, rw_ln_w, rw_ln_b,
              ssm_conv_w, ssm_conv_b, ssm_dt_bias, ssm_a_log, ssm_d, ssm_norm, proj_a, proj_b, w_out,
              router, exp_w1, exp_w3, exp_w2):
    rows = x.shape[1] // GRID_W
    for l in range(DEPTH):
        ctx_out = l < DEPTH - 1
        p = dict(w_in=w_in[l], rw_mu=rw_mu[l], rw_w0=rw_w0[l], rw_w2=rw_w2[l], rw_a0=rw_a0[l], rw_a2=rw_a2[l],
                 rw_g2=rw_g2[l], rw_kk=rw_kk[l], rw_ka=rw_ka[l], rw_rk=rw_rk[l], rw_ln_w=rw_ln_w[l], rw_ln_b=rw_ln_b[l],
                 ssm_conv_w=ssm_conv_w[l], ssm_conv_b=ssm_conv_b[l], ssm_dt_bias=ssm_dt_bias[l], ssm_a_log=ssm_a_log[l],
                 ssm_d=ssm_d[l], ssm_norm=ssm_norm[l], proj_a=proj_a[l], proj_b=proj_b[l], w_out=w_out[l])
        sh1, sc1, g1, sh2, sc2, g2 = adaln_mod(c, ada_w[l], ada_b[l])
        csh1, csc1, cg1, csh2, csc2, cg2 = adaln_mod(c_ctx, ada_w[l], ada_b[l])
        hl = rms_norm(x, norm1_pre[l]) * (1 + sc1) + sh1
        hc = rms_norm(ctx, norm1_pre[l]) * (1 + csc1) + csh1
        mc, ml = mixer(hc, hl, p, rows, ctx_out)
        x = x + g1 * rms_norm(ml, norm1_post[l])
        h2 = rms_norm(x, norm2_pre[l]) * (1 + sc2) + sh2
        x = x + g2 * rms_norm(ec_moe(h2, router[l], exp_w1[l], exp_w3[l], exp_w2[l]), norm2_post[l])
        if ctx_out:
            ctx = ctx + cg1 * rms_norm(mc, norm1_post[l])
            hc2 = rms_norm(ctx, norm2_pre[l]) * (1 + csc2) + csh2
            ctx = ctx + cg2 * rms_norm(ec_moe(hc2, router[l], exp_w1[l], exp_w3[l], exp_w2[l]), norm2_post[l])
    return x
```

```python
import functools
import math

import jax
import jax.numpy as jnp
from jax import lax
from jax.experimental import pallas as pl
from jax.experimental.pallas import tpu as pltpu

F32 = jnp.float32
BF16 = jnp.bfloat16

GRID_W = 64
NORM_EPS = 1e-6
RW_GN_EPS = 64e-5
HEAD = 64
LANES = 128
RW_CHUNK = 64
VMEM_LIMIT = 48 * 1024 * 1024


def _dot(a, b):
    return jnp.dot(a.astype(BF16), b.astype(BF16), preferred_element_type=F32)


def _dot_nt(a, b):
    return lax.dot_general(a.astype(BF16), b.astype(BF16), (((1,), (1,)), ((), ())), preferred_element_type=F32)


def _split3(x):
    hi = x.astype(BF16)
    r1 = x - hi.astype(F32)
    mid = r1.astype(BF16)
    lo = (r1 - mid.astype(F32)).astype(BF16)
    return hi, mid, lo


def _dot_exact_lhs(m, x):
    mb = m.astype(BF16)
    hi, mid, lo = _split3(x)
    return (jnp.dot(mb, hi, preferred_element_type=F32) + jnp.dot(mb, mid, preferred_element_type=F32)
            + jnp.dot(mb, lo, preferred_element_type=F32))


def _rwkv_scan_kernel(r_ref, v_ref, kk_ref, lw_ref, b_ref, k_ref, y_ref, st_ref, *, chunks):
    T = RW_CHUNK
    rev = pl.program_id(2) == 1

    @pl.when(pl.program_id(3) == 0)
    def _():
        st_ref[...] = jnp.zeros_like(st_ref)

    row = lax.broadcasted_iota(jnp.int32, (2 * T, 2 * T), 0)
    col = lax.broadcasted_iota(jnp.int32, (2 * T, 2 * T), 1)
    same = (row >> 6) == (col >> 6)
    rt = row & (T - 1)
    ct = col & (T - 1)
    rt = jnp.where(rev, T - 1 - rt, rt)
    ct = jnp.where(rev, T - 1 - ct, ct)
    m_strict = same & (rt > ct)
    m_incl = same & (rt >= ct)
    eye = (row == col).astype(F32)
    lvl = []
    s = 1
    while s < T:
        sh = s.bit_length() - 1
        br, bc = rt >> sh, ct >> sh
        lvl.append(same & ((br >> 1) == (bc >> 1)) & ((br & 1) == 1) & ((bc & 1) == 0))
        s *= 2
    cum_m = m_incl[:T, :T].astype(F32)
    lane = lax.broadcasted_iota(jnp.int32, (T, LANES), 1)
    head0 = lane < HEAD
    h0f = head0.astype(F32)
    h1f = 1.0 - h0f

    for j in range(chunks):
        cj = jnp.where(rev, chunks - 1 - j, j)
        sl = pl.ds(pl.multiple_of(cj * T, T), T)
        r, v, kk = r_ref[sl, :], v_ref[sl, :], kk_ref[sl, :]
        lw, b, k = lw_ref[sl, :], b_ref[sl, :], k_ref[sl, :]
        c = _dot_exact_lhs(cum_m, lw)
        clast = jnp.sum(lw, axis=0, keepdims=True)
        e_c = jnp.exp(c)
        e_cp = jnp.exp(c - lw)
        e_nc = jnp.exp(-c)
        gl = jnp.exp(clast)
        at = -kk * e_cp
        rtl = r * e_c
        bt = b * e_nc
        kt = k * e_nc
        lhs = jnp.concatenate([at * h0f, at * h1f, rtl * h0f, rtl * h1f], axis=0)
        rhs = jnp.concatenate([bt, bt, kt, kt], axis=0)
        aa = _dot_nt(lhs, rhs)
        n_bd = jnp.where(m_strict, aa[:2 * T, :2 * T], 0.0)
        ak_bd = jnp.where(m_strict, aa[:2 * T, 2 * T:], 0.0)
        rb_bd = jnp.where(m_incl, aa[2 * T:, :2 * T], 0.0)
        rk_bd = jnp.where(m_incl, aa[2 * T:, 2 * T:], 0.0)
        p = eye + jnp.where(lvl[0], n_bd, 0.0)
        for m in lvl[1:]:
            p = p + _dot(_dot(p, jnp.where(m, n_bd, 0.0)), p)
        v_st = jnp.concatenate([v, v], axis=0)
        akv = _dot(ak_bd, v_st)
        st = st_ref[...]
        x = _dot_nt(jnp.concatenate([at, rtl], axis=0), st)
        xa, xr = x[:T], x[T:]
        u_st = _dot(p, jnp.concatenate([xa, xa], axis=0) + akv)
        y_st = _dot(jnp.concatenate([rb_bd, rk_bd], axis=1), jnp.concatenate([u_st, v_st], axis=0))
        y_ref[sl, :] = jnp.where(head0, y_st[:T], y_st[T:]) + xr
        u = jnp.where(head0, u_st[:T], u_st[T:])
        uv_t = jnp.concatenate([u, v], axis=0).T
        upd = _dot(uv_t, jnp.concatenate([bt * gl, kt * gl], axis=0))
        st_ref[...] = jnp.where(same, st * gl + upd, 0.0)


def _rwkv_scan(r, v, kk, lw, b2, k2, lc, chunks):
    B, L, C = r.shape
    tb = chunks * RW_CHUNK
    assert L % tb == 0 and lc % tb == 0 and C % LANES == 0
    nblk, nctx = L // tb, lc // tb

    def blk(d, i):
        bwd = jnp.where(i < nctx, nctx - 1 - i, nblk - 1 - (i - nctx))
        return jnp.where(d == 0, i, bwd)

    shared = pl.BlockSpec((None, tb, LANES), lambda b, p, d, i: (b, blk(d, i), p))
    perdir = pl.BlockSpec((None, None, tb, LANES), lambda b, p, d, i: (d, b, blk(d, i), p))
    return pl.pallas_call(
        functools.partial(_rwkv_scan_kernel, chunks=chunks),
        grid=(B, C // LANES, 2, nblk),
        in_specs=[shared, shared, shared, perdir, perdir, perdir],
        out_specs=perdir,
        out_shape=jax.ShapeDtypeStruct((2, B, L, C), F32),
        scratch_shapes=[pltpu.VMEM((LANES, LANES), F32)],
        compiler_params=pltpu.CompilerParams(
            dimension_semantics=("parallel", "parallel", "arbitrary", "arbitrary"),
            vmem_limit_bytes=VMEM_LIMIT),
        name="rwkv_scan",
    )(r, v, kk, lw, b2, k2)


def _matmul_kernel(a_ref, w_ref, o_ref):
    o_ref[...] = jnp.dot(a_ref[...], w_ref[...], preferred_element_type=F32).astype(o_ref.dtype)


def _matmul(a, w, tm, tn, out_dtype=F32):
    M, K = a.shape
    N = w.shape[1]
    assert M % tm == 0 and N % tn == 0
    return pl.pallas_call(
        _matmul_kernel,
        grid=(M // tm, N // tn),
        in_specs=[pl.BlockSpec((tm, K), lambda i, j: (i, 0)), pl.BlockSpec((K, tn), lambda i, j: (0, j))],
        out_specs=pl.BlockSpec((tm, tn), lambda i, j: (i, j)),
        out_shape=jax.ShapeDtypeStruct((M, N), out_dtype),
        compiler_params=pltpu.CompilerParams(dimension_semantics=("parallel", "parallel"),
                                             vmem_limit_bytes=VMEM_LIMIT),
        name="matmul",
    )(a, w)


def _rms(x, g):
    return x * lax.rsqrt(jnp.mean(x * x, axis=-1, keepdims=True) + NORM_EPS) * g


def _to_cm(t, rows):
    b, n, ch = t.shape
    return t.reshape(b, rows, GRID_W, ch).swapaxes(1, 2).reshape(b, n, ch)


def _from_cm(t, rows):
    b, n, ch = t.shape
    return t.reshape(b, GRID_W, rows, ch).swapaxes(1, 2).reshape(b, n, ch)


def _seg_flip(t, lc):
    return jnp.concatenate([jnp.flip(t[:, :lc], 1), jnp.flip(t[:, lc:], 1)], 1)


def _seg_shift(u, mu, lc):
    L = u.shape[1]
    t = jnp.arange(L)[None, :, None]
    prev = jnp.where((t == 0) | (t == lc), 0.0, jnp.roll(u, 1, axis=1))
    nxt = jnp.where((t == lc - 1) | (t == L - 1), 0.0, jnp.roll(u, -1, axis=1))
    return u + mu[0] * (prev - u) + mu[1] * (nxt - u)


def _rwkv_branch(proj, p, lc, chunks):
    B, L, _ = proj.shape
    u = _seg_shift(proj, p['rw_mu'], lc)
    r, k, v = u[..., :1024], u[..., 1024:2048], u[..., 2048:3072]
    xw = u[..., 3072:3200].reshape(B, L, 2, 64)
    xa = u[..., 3200:3328].reshape(B, L, 2, 64)
    xg = u[..., 3328:3456]
    w_raw = p['rw_w0'][:, None, None] + jnp.einsum('blde,def->dblf', jnp.tanh(xw), p['rw_w2'])
    logw = -math.exp(-0.5) * jax.nn.sigmoid(w_raw)
    a = jax.nn.sigmoid(p['rw_a0'][:, None, None] + jnp.einsum('blde,def->dblf', xa, p['rw_a2']))
    g = jnp.einsum('ble,ef->blf', jax.nn.sigmoid(xg), p['rw_g2'])
    kk = (k * p['rw_kk']).reshape(B, L, 16, 64)
    kk = (kk * lax.rsqrt(jnp.maximum(jnp.sum(kk * kk, -1, keepdims=True), 1e-12))).reshape(B, L, 1024)
    k_eff = k[None] * (1 + (a - 1) * p['rw_ka'])
    y2 = _rwkv_scan(r, v, kk, logw, kk[None] * a, k_eff, lc, chunks)
    y = (y2[0] + y2[1])[:, lc:].reshape(B, L - lc, 16, 64)
    mean = jnp.mean(y, -1, keepdims=True)
    var = jnp.mean(jnp.square(y - mean), -1, keepdims=True)
    yn = ((y - mean) * lax.rsqrt(var + RW_GN_EPS)).reshape(B, L - lc, 1024) * p['rw_ln_w'] + p['rw_ln_b']
    ksum = k_eff[0, :, lc:] + k_eff[1, :, lc:]
    bonus = jnp.sum((r[:, lc:] * ksum * p['rw_rk'].reshape(1024)).reshape(B, L - lc, 16, 64), -1, keepdims=True)
    return (yn + (bonus * v[:, lc:].reshape(B, L - lc, 16, 64)).reshape(B, L - lc, 1024)) * g[:, lc:]


def _seg_conv(u, w, b, lc):
    L = u.shape[1]
    t = jnp.arange(L)[None, :, None]
    seg = t >= lc
    out = jnp.zeros_like(u) + b
    for j in range(5):
        off = j - 2
        src = t + off
        ok = (src >= 0) & (src < L) & ((src >= lc) == seg)
        out = out + jnp.where(ok, jnp.roll(u, -off, axis=1), 0.0) * w[:, j]
    return out


def _ssd_scan(xs, dt, bm, cm_, a_head, reverse, lc, Q=128):
    seq = (xs, dt, bm, cm_)
    if reverse:
        seq = tuple(_seg_flip(t, lc) for t in seq)
    xs, dt, bm, cm_ = seq
    B, L, H, P = xs.shape
    G, N = bm.shape[2:]
    E = H // G
    nc = L // Q
    ch = lambda t: jnp.moveaxis(t.reshape(B, nc, Q, *t.shape[2:]), 1, 0)
    xg = ch(xs.reshape(B, L, G, E, P))
    dtg = ch(dt.reshape(B, L, G, E))
    bg, cg = ch(bm), ch(cm_)
    ag = a_head.reshape(G, E)
    lower = jnp.tril(jnp.ones((Q, Q), bool))[None, :, :, None, None]

    def step(h, inp):
        x_c, dt_c, b_c, c_c = inp
        cum = jnp.cumsum(dt_c * ag, axis=1)
        xdt = x_c * dt_c[..., None]
        h_new = h * jnp.exp(cum[:, -1])[..., None, None] + jnp.einsum('bjgn,bjge,bjgep->bgepn', b_c, jnp.exp(cum[:, -1:] - cum), xdt)
        seg = jnp.exp(jnp.where(lower, cum[:, :, None] - cum[:, None, :], -jnp.inf))
        cb = jnp.einsum('bign,bjgn->bijg', c_c, b_c)
        y_intra = jnp.einsum('bijg,bijge,bjgep->bigep', cb, seg, xdt)
        y_inter = jnp.einsum('bign,bgepn->bigep', c_c, h) * jnp.exp(cum)[..., None]
        return h_new, y_intra + y_inter

    h0 = jnp.zeros((B, G, E, P, N), F32)
    _, ys = lax.scan(step, h0, (xg, dtg, bg, cg))
    y = jnp.moveaxis(ys, 0, 1).reshape(B, L, H, P)
    if reverse:
        y = _seg_flip(y, lc)
    return y


def _ssm_branch(zx, dtp, p, lc):
    B, L, _ = zx.shape
    z, xbc = zx[..., :2048], zx[..., 2048:5120]
    xbc = jax.nn.silu(_seg_conv(xbc, p['ssm_conv_w'], p['ssm_conv_b'], lc))
    xs = xbc[..., :2048].reshape(B, L, 32, 64)
    bm = xbc[..., 2048:2560].reshape(B, L, 4, 128)
    cm_ = xbc[..., 2560:3072].reshape(B, L, 4, 128)
    dt = jax.nn.softplus(dtp.reshape(B, L, 2, 32) + p['ssm_dt_bias'])
    y = 0.0
    for d in (0, 1):
        y = y + _ssd_scan(xs, dt[:, :, d], bm, cm_, -jnp.exp(p['ssm_a_log'][d]), d == 1, lc)
    y = (y + p['ssm_d'][:, None] * xs).reshape(B, L, 2048)[:, lc:]
    return _rms(y * jax.nn.silu(z[:, lc:]), p['ssm_norm'])


def _merge(ya, yb, gates, p):
    ga, gb = jnp.split(jax.nn.sigmoid(gates), 2, axis=-1)
    return jnp.matmul(ga * jnp.matmul(ya, p['proj_a']) + gb * jnp.matmul(yb, p['proj_b']), p['w_out'])


def _ec_moe(h, router, w1, w3, w2):
    b_, n, _ = h.shape
    cap = 2 * n // 16
    aff = jax.nn.softmax(jnp.matmul(h, router), axis=-1)
    gate, idx = lax.top_k(jnp.swapaxes(aff, 1, 2), cap)
    bidx = jnp.arange(b_)[:, None, None]
    xe = h[bidx, idx]
    hid = jax.nn.silu(jnp.einsum('becd,edf->becf', xe, w1)) * jnp.einsum('becd,edf->becf', xe, w3)
    ye = jnp.einsum('becf,efd->becd', hid, w2) * gate[..., None]
    return jnp.zeros_like(h).at[bidx, idx].add(ye)


def _adaln(cond, w, b):
    return jnp.matmul(jax.nn.silu(cond), w) + b


def kernel(x, c, ctx, c_ctx, ada_w, ada_b, norm1_pre, norm1_post, norm2_pre, norm2_post,
           w_in, rw_mu, rw_w0, rw_w2, rw_a0, rw_a2, rw_g2, rw_kk, rw_ka, rw_rk, rw_ln_w, rw_ln_b,
           ssm_conv_w, ssm_conv_b, ssm_dt_bias, ssm_a_log, ssm_d, ssm_norm, proj_a, proj_b, w_out,
           router, exp_w1, exp_w3, exp_w2):
    B, S, D = x.shape
    lc = ctx.shape[1]
    L = lc + S
    rows = S // GRID_W
    assert ada_w.shape[0] == 1, "single trunk layer"
    l = 0
    p = dict(rw_mu=rw_mu[l], rw_w0=rw_w0[l], rw_w2=rw_w2[l], rw_a0=rw_a0[l], rw_a2=rw_a2[l],
             rw_g2=rw_g2[l], rw_kk=rw_kk[l], rw_ka=rw_ka[l], rw_rk=rw_rk[l], rw_ln_w=rw_ln_w[l], rw_ln_b=rw_ln_b[l],
             ssm_conv_w=ssm_conv_w[l], ssm_conv_b=ssm_conv_b[l], ssm_dt_bias=ssm_dt_bias[l], ssm_a_log=ssm_a_log[l],
             ssm_d=ssm_d[l], ssm_norm=ssm_norm[l], proj_a=proj_a[l], proj_b=proj_b[l], w_out=w_out[l])
    m = _adaln(c, ada_w[l], ada_b[l]).reshape(B, 6, 1, D)
    cmod = _adaln(c_ctx, ada_w[l], ada_b[l]).reshape(6, D)
    hl = (_rms(x, norm1_pre[l]) * (1 + m[:, 1]) + m[:, 0]).astype(BF16)
    hc = (_rms(ctx, norm1_pre[l]) * (1 + cmod[1]) + cmod[0]).astype(BF16)
    seq_a = jnp.concatenate([hc, hl], 1).reshape(B * L, D)
    seq_b = jnp.concatenate([hc, _to_cm(hl, rows)], 1).reshape(B * L, D)
    w = w_in[l].astype(BF16)
    tm = 512 if (B * L) % 512 == 0 else 128
    proj_rw = _matmul(seq_a, w[:, :3456], tm, 1152).reshape(B, L, 3456)
    gates = _matmul(hl.reshape(B * S, D), w[:, 8640:], tm, 1024).reshape(B, S, 2048)
    proj_zx = _matmul(seq_b, w[:, 3456:8576], tm, 1024).reshape(B, L, 5120)
    w_dt = jnp.pad(w[:, 8576:8640], ((0, 0), (0, 64)))
    proj_dt = _matmul(seq_b, w_dt, tm, 128).reshape(B, L, 128)[..., :64]
    chunks = 4 if lc % 256 == 0 else 2
    ya = _rwkv_branch(proj_rw, p, lc, chunks)
    yb = _from_cm(_ssm_branch(proj_zx, proj_dt, p, lc), rows)
    ml = _merge(ya, yb, gates, p)
    x1 = x + m[:, 2] * _rms(ml, norm1_post[l])
    h2 = _rms(x1, norm2_pre[l]) * (1 + m[:, 4]) + m[:, 3]
    return x1 + m[:, 5] * _rms(_ec_moe(h2, router[l], exp_w1[l], exp_w3[l], exp_w2[l]), norm2_post[l])
```

```python
import functools
import math

import jax
import jax.numpy as jnp
from jax import lax
from jax.experimental import pallas as pl
from jax.experimental.pallas import tpu as pltpu

F32 = jnp.float32
BF16 = jnp.bfloat16

GRID_W = 64
NORM_EPS = 1e-6
RW_GN_EPS = 64e-5
HEAD = 64
LANES = 128
HALO = 8
RW_CHUNK = 64
RW_LORA = 3 * LANES
SSM_TAPS = 5
SSD_CHUNK = 128
SSM_HEADS, SSM_GROUPS, SSM_STATE = 32, 4, 128
VMEM_LIMIT = 48 * 1024 * 1024
MOE_VMEM_LIMIT = 56 * 1024 * 1024


def _dot(a, b):
    return jnp.dot(a.astype(BF16), b.astype(BF16), preferred_element_type=F32)


def _dot_nt(a, b):
    return lax.dot_general(a.astype(BF16), b.astype(BF16), (((1,), (1,)), ((), ())), preferred_element_type=F32)


def _split3(x):
    hi = x.astype(BF16)
    r1 = x - hi.astype(F32)
    mid = r1.astype(BF16)
    lo = (r1 - mid.astype(F32)).astype(BF16)
    return hi, mid, lo


def _dot_exact_lhs(m, x):
    mb = m.astype(BF16)
    hi, mid, lo = _split3(x)
    return (jnp.dot(mb, hi, preferred_element_type=F32) + jnp.dot(mb, mid, preferred_element_type=F32)
            + jnp.dot(mb, lo, preferred_element_type=F32))


def _dot3(x, m):
    mb = m.astype(BF16)
    hi, mid, lo = _split3(x)
    return (jnp.dot(hi, mb, preferred_element_type=F32) + jnp.dot(mid, mb, preferred_element_type=F32)
            + jnp.dot(lo, mb, preferred_element_type=F32))


def _sigmoid(x):
    return 1.0 / (1.0 + jnp.exp(-x))


def _softplus(x):
    return jnp.maximum(x, 0.0) + jnp.log(1.0 + jnp.exp(-jnp.abs(x)))


def _head_sum(x, ones_bd):
    n = x.shape[1] // LANES
    return jnp.concatenate([_dot3(x[:, i * LANES:(i + 1) * LANES], ones_bd) for i in range(n)], axis=1)


def _head_ones():
    i = jnp.arange(LANES)
    return ((i[:, None] // HEAD) == (i[None, :] // HEAD)).astype(F32)


def _halo_specs(tt, width, col, L):
    per = tt // HALO
    nh = L // HALO
    main = pl.BlockSpec((None, tt, width), lambda b, i, *_: (b, i, col(*_)))
    prev = pl.BlockSpec((None, HALO, width), lambda b, i, *_: (b, jnp.maximum(i * per - 1, 0), col(*_)))
    nxt = pl.BlockSpec((None, HALO, width), lambda b, i, *_: (b, jnp.minimum((i + 1) * per, nh - 1), col(*_)))
    return main, prev, nxt


def _seg_edges(i, tt, lc, L):
    t0 = i * tt
    has_prev = jnp.logical_and(t0 != 0, t0 != lc)
    has_next = jnp.logical_and(t0 + tt != lc, t0 + tt != L)
    return has_prev.astype(F32), has_next.astype(F32)


def _adaln_kernel(c_ref, w_ref, b_ref, o_ref):
    c = c_ref[...]
    o_ref[...] = jnp.dot(c * _sigmoid(c), w_ref[...], preferred_element_type=F32,
                         precision=lax.Precision.HIGHEST) + b_ref[...]


def _adaln(cond, w, b, tn):
    M, D = cond.shape
    N = w.shape[1]
    return pl.pallas_call(
        _adaln_kernel,
        grid=(N // tn,),
        in_specs=[pl.BlockSpec((M, D), lambda j: (0, 0)), pl.BlockSpec((D, tn), lambda j: (0, j)),
                  pl.BlockSpec((1, tn), lambda j: (0, j))],
        out_specs=pl.BlockSpec((M, tn), lambda j: (0, j)),
        out_shape=jax.ShapeDtypeStruct((M, N), F32),
        compiler_params=pltpu.CompilerParams(dimension_semantics=("parallel",), vmem_limit_bytes=VMEM_LIMIT),
        name="adaln",
    )(cond, w, b.reshape(1, N))


def _prenorm_kernel(x_ref, mod_ref, nw_ref, o_ref):
    x = x_ref[...]
    y = x * lax.rsqrt(jnp.mean(x * x, axis=-1, keepdims=True) + NORM_EPS) * nw_ref[...]
    o_ref[...] = (y * (1.0 + mod_ref[1:2, :]) + mod_ref[0:1, :]).astype(o_ref.dtype)


def _prenorm(x, mods, nw, tm):
    B, N, D = x.shape
    per_sample = mods.shape[0] == B
    tok = pl.BlockSpec((None, tm, D), lambda b, i: (b, i, 0))
    return pl.pallas_call(
        _prenorm_kernel,
        grid=(B, N // tm),
        in_specs=[tok, pl.BlockSpec((None, 6, D), (lambda b, i: (b, 0, 0)) if per_sample else (lambda b, i: (0, 0, 0))),
                  pl.BlockSpec((1, D), lambda b, i: (0, 0))],
        out_specs=tok,
        out_shape=jax.ShapeDtypeStruct((B, N, D), BF16),
        compiler_params=pltpu.CompilerParams(dimension_semantics=("parallel", "parallel"),
                                             vmem_limit_bytes=VMEM_LIMIT),
        name="prenorm",
    )(x, mods, nw.reshape(1, D))


def _matmul_kernel(a_ref, w_ref, o_ref):
    o_ref[...] = jnp.dot(a_ref[...], w_ref[...], preferred_element_type=F32).astype(o_ref.dtype)


def _matmul(a, w, tm, tn, out_dtype=F32):
    M, K = a.shape
    N = w.shape[1]
    assert M % tm == 0 and N % tn == 0
    return pl.pallas_call(
        _matmul_kernel,
        grid=(M // tm, N // tn),
        in_specs=[pl.BlockSpec((tm, K), lambda i, j: (i, 0)), pl.BlockSpec((K, tn), lambda i, j: (0, j))],
        out_specs=pl.BlockSpec((tm, tn), lambda i, j: (i, j)),
        out_shape=jax.ShapeDtypeStruct((M, N), out_dtype),
        compiler_params=pltpu.CompilerParams(dimension_semantics=("parallel", "parallel"),
                                             vmem_limit_bytes=VMEM_LIMIT),
        name="matmul",
    )(a, w)


def _rwkv_prep_kernel(p_ref, pp_ref, pn_ref, mu_ref, w2_ref, a2_ref, g2_ref, w0_ref, a0_ref, kkw_ref, ka_ref,
                      rk_ref, ones_ref, r_ref, v_ref, kk_ref, lw_ref, b_ref, ke_ref, g_ref, bv_ref, *, tt, lc, L):
    has_prev, has_next = _seg_edges(pl.program_id(1), tt, lc, L)
    rows = lax.broadcasted_iota(jnp.int32, (tt, 1), 0)
    first, last = rows == 0, rows == tt - 1

    def shifted(lo, hi):
        u = p_ref[:, lo:hi]
        pv = pp_ref[HALO - 1:HALO, lo:hi] * has_prev
        nx = pn_ref[0:1, lo:hi] * has_next
        up = jnp.where(first, pv, pltpu.roll(u, 1, 0))
        un = jnp.where(last, nx, pltpu.roll(u, tt - 1, 0))
        return u + mu_ref[0:1, lo:hi] * (up - u) + mu_ref[1:2, lo:hi] * (un - u)

    C = r_ref.shape[-1]
    lane = lax.broadcasted_iota(jnp.int32, (1, LANES), 1)
    half = [(lane < HEAD).astype(F32), (lane >= HEAD).astype(F32)]
    xw = jnp.tanh(shifted(3 * C, 3 * C + LANES))
    xa = shifted(3 * C + LANES, 3 * C + 2 * LANES)
    xg = _sigmoid(shifted(3 * C + 2 * LANES, 3 * C + 3 * LANES))
    g_ref[...] = _dot(xg, g2_ref[...])
    r = shifted(0, C)
    k = shifted(C, 2 * C)
    v = shifted(2 * C, 3 * C)
    r_ref[...] = r
    v_ref[...] = v
    ones_bd = ones_ref[...]
    kkr = k * kkw_ref[...]
    kk = kkr * lax.rsqrt(jnp.maximum(_head_sum(kkr * kkr, ones_bd), 1e-12))
    kk_ref[...] = kk
    ksum = jnp.zeros_like(k)
    for d in (0, 1):
        w_raw = w0_ref[d:d + 1, :] + _dot(xw * half[d], w2_ref[...])
        lw_ref[d] = -math.exp(-0.5) * _sigmoid(w_raw)
        a = _sigmoid(a0_ref[d:d + 1, :] + _dot(xa * half[d], a2_ref[...]))
        b_ref[d] = kk * a
        ke = k * (1.0 + (a - 1.0) * ka_ref[...])
        ke_ref[d] = ke
        ksum = ksum + ke
    bv_ref[...] = _head_sum(r * ksum * rk_ref[...], ones_bd) * v


def _rwkv_prep(proj, p, lc, tt):
    B, L, W = proj.shape
    C = (W - RW_LORA) // 3
    assert L % tt == 0 and lc % tt == 0
    main, prev, nxt = _halo_specs(tt, W, lambda: 0, L)
    full = lambda a: pl.BlockSpec(a.shape, lambda b, i: (0,) * a.ndim)
    w2 = p['rw_w2'].reshape(2 * HEAD, C)
    a2 = p['rw_a2'].reshape(2 * HEAD, C)
    params = [p['rw_mu'], w2, a2, p['rw_g2'], p['rw_w0'], p['rw_a0'], p['rw_kk'].reshape(1, C),
              p['rw_ka'].reshape(1, C), p['rw_rk'].reshape(1, C), _head_ones()]
    one = pl.BlockSpec((None, tt, C), lambda b, i: (b, i, 0))
    two = pl.BlockSpec((2, None, tt, C), lambda b, i: (0, b, i, 0))
    s1 = jax.ShapeDtypeStruct((B, L, C), F32)
    s2 = jax.ShapeDtypeStruct((2, B, L, C), F32)
    return pl.pallas_call(
        functools.partial(_rwkv_prep_kernel, tt=tt, lc=lc, L=L),
        grid=(B, L // tt),
        in_specs=[main, prev, nxt] + [full(a) for a in params],
        out_specs=[one, one, one, two, two, two, one, one],
        out_shape=[s1, s1, s1, s2, s2, s2, s1, s1],
        compiler_params=pltpu.CompilerParams(dimension_semantics=("parallel", "parallel"),
                                             vmem_limit_bytes=VMEM_LIMIT),
        name="rwkv_prep",
    )(proj, proj, proj, *params)


def _rwkv_masks():
    T = RW_CHUNK
    idx = jnp.arange(2 * T)
    same = (idx[:, None] // T) == (idx[None, :] // T)
    out = []
    for rev in (False, True):
        t = idx % T
        t = (T - 1 - t) if rev else t
        rt, ct = t[:, None], t[None, :]
        ms = [same & (rt > ct), same & (rt >= ct)]
        s = 1
        while s < T:
            br, bc = rt // s, ct // s
            ms.append(same & (br // 2 == bc // 2) & (br % 2 == 1) & (bc % 2 == 0))
            s *= 2
        out.append(jnp.stack(ms))
    return jnp.stack(out).astype(F32)


def _rwkv_scan_kernel(m_ref, rf_ref, vf_ref, kkf_ref, rb_ref, vb_ref, kkb_ref, lwf_ref, bf_ref, kf_ref,
                      lwb_ref, bb_ref, kb_ref, yf_ref, yb_ref, st_ref, *, chunks, pairs):
    T = RW_CHUNK

    @pl.when(pl.program_id(2) == 0)
    def _():
        st_ref[...] = jnp.zeros_like(st_ref)

    row = lax.broadcasted_iota(jnp.int32, (2 * T, 2 * T), 0)
    col = lax.broadcasted_iota(jnp.int32, (2 * T, 2 * T), 1)
    eye = (row == col).astype(F32)
    bd = ((row >> 6) == (col >> 6)).astype(F32)
    head0 = lax.broadcasted_iota(jnp.int32, (T, LANES), 1) < HEAD
    h0f = head0.astype(F32)
    h1f = 1.0 - h0f
    refs = ((rf_ref, vf_ref, kkf_ref, lwf_ref, bf_ref, kf_ref, yf_ref),
            (rb_ref, vb_ref, kkb_ref, lwb_ref, bb_ref, kb_ref, yb_ref))
    chains = [(d, q) for d in (0, 1) for q in range(pairs)]
    order = {0: list(range(chunks)), 1: list(range(chunks - 1, -1, -1))}
    units = [(d, q, j) for j in range(chunks) for (d, q) in chains]
    n_levels = RW_CHUNK.bit_length() - 1

    def tile(ref, d, q, j):
        cj = order[d][j]
        return ref[cj * T:(cj + 1) * T, q * LANES:(q + 1) * LANES]

    A = {}
    for un in units:
        d, q, j = un
        r_ref, v_ref, kk_ref, lw_ref, b_ref, k_ref, _ = refs[d]
        lw = tile(lw_ref, d, q, j)
        c = _dot_exact_lhs(m_ref[d, 1, :T, :T], lw)
        gl = jnp.exp(jnp.sum(lw, axis=0, keepdims=True))
        e_nc = jnp.exp(-c)
        at = -tile(kk_ref, d, q, j) * jnp.exp(c - lw)
        rtl = tile(r_ref, d, q, j) * jnp.exp(c)
        bt = tile(b_ref, d, q, j) * e_nc
        kt = tile(k_ref, d, q, j) * e_nc
        A[un] = dict(gl=gl, at=at, rtl=rtl, bt=bt, kt=kt)
    for un in units:
        a = A[un]
        lhs = jnp.concatenate([a['at'] * h0f, a['at'] * h1f, a['rtl'] * h0f, a['rtl'] * h1f], axis=0)
        rhs = jnp.concatenate([a['bt'], a['bt'], a['kt'], a['kt']], axis=0)
        a['aa'] = _dot_nt(lhs, rhs)
    for un in units:
        d = un[0]
        a = A[un]
        aa = a.pop('aa')
        a['n'] = aa[:2 * T, :2 * T] * m_ref[d, 0]
        a['ak'] = (aa[:2 * T, 2 * T:] * m_ref[d, 0]).astype(BF16)
        a['rbrk'] = jnp.concatenate([aa[2 * T:, :2 * T] * m_ref[d, 1], aa[2 * T:, 2 * T:] * m_ref[d, 1]],
                                    axis=1).astype(BF16)
        a['p'] = eye + a['n'] * m_ref[d, 2]
    for lv in range(1, n_levels):
        for un in units:
            a = A[un]
            a['pn'] = _dot(a['p'], a['n'] * m_ref[un[0], 2 + lv])
        for un in units:
            a = A[un]
            a['p'] = a['p'] + _dot(a.pop('pn'), a['p'])
    for un in units:
        d, q, j = un
        a = A[un]
        v = tile(refs[d][1], d, q, j)
        a['akv'] = _dot(a.pop('ak'), jnp.concatenate([v, v], axis=0))
        a['p'] = a['p'].astype(BF16)
        a['ar'] = jnp.concatenate([a.pop('at'), a.pop('rtl')], axis=0).astype(BF16)
        a['bk'] = jnp.concatenate([a.pop('bt') * a['gl'], a.pop('kt') * a['gl']], axis=0).astype(BF16)
        a.pop('n')

    st = {ch: st_ref[ci] for ci, ch in enumerate(chains)}
    for j in range(chunks):
        x = {}
        for ch in chains:
            x[ch] = _dot_nt(A[ch + (j,)]['ar'], st[ch])
        u_st = {}
        for ch in chains:
            a = A[ch + (j,)]
            xa = x[ch][:T]
            u_st[ch] = _dot(a['p'], jnp.concatenate([xa, xa], axis=0) + a['akv'])
        for ch in chains:
            d, q = ch
            a = A[ch + (j,)]
            v = tile(refs[d][1], d, q, j)
            y_st = _dot(a['rbrk'], jnp.concatenate([u_st[ch], jnp.concatenate([v, v], axis=0)], axis=0))
            cj = order[d][j]
            refs[d][6][cj * T:(cj + 1) * T, q * LANES:(q + 1) * LANES] = (
                jnp.where(head0, y_st[:T], y_st[T:]) + x[ch][T:])
            u = jnp.where(head0, u_st[ch][:T], u_st[ch][T:])
            upd = _dot(jnp.concatenate([u, v], axis=0).T, a['bk'])
            st[ch] = (st[ch] * a['gl'] + upd) * bd
    for ci, ch in enumerate(chains):
        st_ref[ci] = st[ch]


def _rwkv_scan(r, v, kk, lw, b2, k2, lc, chunks, pairs):
    B, L, C = r.shape
    tb = chunks * RW_CHUNK
    wl = pairs * LANES
    assert L % tb == 0 and lc % tb == 0 and C % wl == 0
    nblk, nctx = L // tb, lc // tb

    def bwd(i):
        return jnp.where(i < nctx, nctx - 1 - i, nblk - 1 - (i - nctx))

    sf = pl.BlockSpec((None, tb, wl), lambda b, p, i: (b, i, p))
    sb = pl.BlockSpec((None, tb, wl), lambda b, p, i: (b, bwd(i), p))
    df = pl.BlockSpec((None, None, tb, wl), lambda b, p, i: (0, b, i, p))
    db = pl.BlockSpec((None, None, tb, wl), lambda b, p, i: (1, b, bwd(i), p))
    masks = _rwkv_masks()
    return pl.pallas_call(
        functools.partial(_rwkv_scan_kernel, chunks=chunks, pairs=pairs),
        grid=(B, C // wl, nblk),
        in_specs=[pl.BlockSpec(masks.shape, lambda b, p, i: (0, 0, 0, 0)),
                  sf, sf, sf, sb, sb, sb, df, df, df, db, db, db],
        out_specs=[sf, sb],
        out_shape=[jax.ShapeDtypeStruct((B, L, C), F32)] * 2,
        scratch_shapes=[pltpu.VMEM((2 * pairs, LANES, LANES), F32)],
        compiler_params=pltpu.CompilerParams(
            dimension_semantics=("parallel", "parallel", "arbitrary"),
            vmem_limit_bytes=VMEM_LIMIT),
        name="rwkv_scan",
    )(masks, r, v, kk, r, v, kk, lw, b2, k2, lw, b2, k2)


def _ssm_conv_kernel(x_ref, xp_ref, xn_ref, w_ref, b_ref, o_ref, *, tt, lc, L):
    has_prev, has_next = _seg_edges(pl.program_id(1), tt, lc, L)
    ext = jnp.concatenate([xp_ref[...] * has_prev, x_ref[...], xn_ref[...] * has_next], axis=0)
    n = tt + 2 * HALO
    acc = jnp.zeros(x_ref.shape, F32) + b_ref[...]
    for j in range(SSM_TAPS):
        off = j - SSM_TAPS // 2
        sh = ext if off == 0 else pltpu.roll(ext, (-off) % n, 0)
        acc = acc + sh[HALO:HALO + tt] * w_ref[j:j + 1, :]
    o_ref[...] = acc * _sigmoid(acc)


def _ssm_conv(zx, w, b, lc, tt, tw, col0):
    B, L, _ = zx.shape
    cc = w.shape[0]
    assert cc % tw == 0 and col0 % tw == 0 and L % tt == 0 and lc % tt == 0
    main, prev, nxt = _halo_specs(tt, tw, lambda c: c + col0 // tw, L)
    wt = jnp.transpose(w)
    return pl.pallas_call(
        functools.partial(_ssm_conv_kernel, tt=tt, lc=lc, L=L),
        grid=(B, L // tt, cc // tw),
        in_specs=[main, prev, nxt, pl.BlockSpec((SSM_TAPS, tw), lambda b, i, c: (0, c)),
                  pl.BlockSpec((1, tw), lambda b, i, c: (0, c))],
        out_specs=pl.BlockSpec((None, tt, tw), lambda b, i, c: (b, i, c)),
        out_shape=jax.ShapeDtypeStruct((B, L, cc), F32),
        compiler_params=pltpu.CompilerParams(dimension_semantics=("parallel", "parallel", "parallel"),
                                             vmem_limit_bytes=VMEM_LIMIT),
        name="ssm_conv",
    )(zx, zx, zx, wt, b.reshape(1, cc))


def _ssd_scan_kernel(x_ref, bm_ref, cm_ref, dt_ref, dtt_ref, a_ref, at_ref, bias_ref, biast_ref, y_ref,
                     ht_ref, cumt_ref, *, hpg):
    Q = SSD_CHUNK
    g = pl.program_id(1)
    rev = pl.program_id(2) == 1

    @pl.when(pl.program_id(3) == 0)
    def _():
        ht_ref[...] = jnp.zeros_like(ht_ref)

    row = lax.broadcasted_iota(jnp.int32, (Q, Q), 0)
    col = lax.broadcasted_iota(jnp.int32, (Q, Q), 1)
    incl = jnp.where(rev, col - row, row - col) >= 0
    tri = incl.astype(F32)
    dt = _softplus(dt_ref[...] + bias_ref[...])
    dta = dt * a_ref[...]
    cum = _dot_exact_lhs(tri, dta)
    dtt = _softplus(dtt_ref[...] + biast_ref[...])
    hi, mid, lo = _split3(dtt * at_ref[...])
    trib = tri.astype(BF16)
    nt = lambda a_, b_: lax.dot_general(a_, b_, (((1,), (1,)), ((), ())), preferred_element_type=F32)
    cumt_ref[...] = nt(hi, trib) + nt(mid, trib) + nt(lo, trib)
    W = hpg * HEAD
    hrow = lax.broadcasted_iota(jnp.int32, (LANES, W), 0)
    hlane = lax.broadcasted_iota(jnp.int32, (LANES, W), 1)
    expand = (hrow == g * hpg + (hlane >> 6)).astype(F32)
    cum_l = _dot3(cum, expand)
    dt_l = _dot3(dt, expand)
    crow = lax.broadcasted_iota(jnp.int32, (LANES, hpg * Q), 0)
    clane = lax.broadcasted_iota(jnp.int32, (LANES, hpg * Q), 1)
    colb = _dot3(cum, (crow == g * hpg + (clane >> 7)).astype(F32))
    bm, cm = bm_ref[...], cm_ref[...]
    cb = _dot_nt(cm, bm)
    xdt = x_ref[...] * dt_l
    clast = jnp.min(cum_l, axis=0, keepdims=True)
    ecum = jnp.exp(cum_l)
    ht = ht_ref[...]
    y_inter = _dot(cm, ht) * ecum
    lane = lax.broadcasted_iota(jnp.int32, (1, LANES), 1)
    h0f = (lane < HEAD).astype(F32)
    h1f = 1.0 - h0f
    ys = []
    for pp in range(hpg // 2):
        ms = []
        for e in (2 * pp, 2 * pp + 1):
            rowe = cumt_ref[pl.ds(g * hpg + e, 1), :]
            seg = jnp.exp(jnp.where(incl, colb[:, e * Q:(e + 1) * Q] - rowe, -jnp.inf))
            ms.append(cb * seg)
        xp = xdt[:, pp * LANES:(pp + 1) * LANES]
        ys.append(_dot(jnp.concatenate(ms, axis=1), jnp.concatenate([xp * h0f, xp * h1f], axis=0)))
    y_ref[...] = jnp.concatenate(ys, axis=1) + y_inter
    ht_ref[...] = ht * jnp.exp(clast) + _dot(bm.T, xdt * jnp.exp(clast - cum_l))


def _ssd_scan(xbc, dtp, p, lc):
    B, L, _ = xbc.shape
    Q = SSD_CHUNK
    H, G, N = SSM_HEADS, SSM_GROUPS, SSM_STATE
    hpg = H // G
    W = hpg * HEAD
    xw = H * HEAD
    assert L % Q == 0 and lc % Q == 0
    nck, nctx = L // Q, lc // Q
    dt = jnp.pad(jnp.moveaxis(dtp.reshape(B, L, 2, H), 2, 1), ((0, 0), (0, 0), (0, 0), (0, LANES - H)))
    dtt = jnp.swapaxes(dt, 2, 3)
    a = jnp.pad(-jnp.exp(p['ssm_a_log']), ((0, 0), (0, LANES - H)))
    bias = jnp.pad(p['ssm_dt_bias'], ((0, 0), (0, LANES - H)))

    def ck(d, i):
        bwd = jnp.where(i < nctx, nctx - 1 - i, nck - 1 - (i - nctx))
        return jnp.where(d == 0, i, bwd)

    return pl.pallas_call(
        functools.partial(_ssd_scan_kernel, hpg=hpg),
        grid=(B, G, 2, nck),
        in_specs=[pl.BlockSpec((None, Q, W), lambda b, g, d, i: (b, ck(d, i), g)),
                  pl.BlockSpec((None, Q, N), lambda b, g, d, i: (b, ck(d, i), xw // N + g)),
                  pl.BlockSpec((None, Q, N), lambda b, g, d, i: (b, ck(d, i), xw // N + G + g)),
                  pl.BlockSpec((None, None, Q, LANES), lambda b, g, d, i: (b, d, ck(d, i), 0)),
                  pl.BlockSpec((None, None, LANES, Q), lambda b, g, d, i: (b, d, 0, ck(d, i))),
                  pl.BlockSpec((None, 1, LANES), lambda b, g, d, i: (d, 0, 0)),
                  pl.BlockSpec((None, LANES, 1), lambda b, g, d, i: (d, 0, 0)),
                  pl.BlockSpec((None, 1, LANES), lambda b, g, d, i: (d, 0, 0)),
                  pl.BlockSpec((None, LANES, 1), lambda b, g, d, i: (d, 0, 0))],
        out_specs=pl.BlockSpec((None, None, Q, W), lambda b, g, d, i: (d, b, ck(d, i), g)),
        out_shape=jax.ShapeDtypeStruct((2, B, L, xw), F32),
        scratch_shapes=[pltpu.VMEM((N, W), F32), pltpu.VMEM((LANES, Q), F32)],
        compiler_params=pltpu.CompilerParams(
            dimension_semantics=("parallel", "parallel", "arbitrary", "arbitrary"),
            vmem_limit_bytes=VMEM_LIMIT),
        name="ssd_scan",
    )(xbc, xbc, xbc, dt, dtt, a.reshape(2, 1, LANES), a.reshape(2, LANES, 1), bias.reshape(2, 1, LANES),
      bias.reshape(2, LANES, 1))


def _ssm_readout_kernel(yf_ref, yb_ref, x_ref, z_ref, d_ref, nw_ref, o_ref):
    z = z_ref[...]
    y = (yf_ref[...] + yb_ref[...] + d_ref[...] * x_ref[...]) * (z * _sigmoid(z))
    o_ref[...] = (y * lax.rsqrt(jnp.mean(y * y, axis=-1, keepdims=True) + NORM_EPS) * nw_ref[...]).astype(o_ref.dtype)


def _ssm_readout(y2, xbc, zx, p, lc, tt):
    _, B, L, C = y2.shape
    S = L - lc
    off = lc // tt
    d_l = jnp.repeat(p['ssm_d'], HEAD).reshape(1, C)
    row = lambda b, i: (b, i + off, 0)
    return pl.pallas_call(
        _ssm_readout_kernel,
        grid=(B, S // tt),
        in_specs=[pl.BlockSpec((None, None, tt, C), lambda b, i: (0, b, i + off, 0)),
                  pl.BlockSpec((None, None, tt, C), lambda b, i: (1, b, i + off, 0)),
                  pl.BlockSpec((None, tt, C), row), pl.BlockSpec((None, tt, C), row),
                  pl.BlockSpec((1, C), lambda b, i: (0, 0)), pl.BlockSpec((1, C), lambda b, i: (0, 0))],
        out_specs=pl.BlockSpec((None, tt, C), lambda b, i: (b, i, 0)),
        out_shape=jax.ShapeDtypeStruct((B, S, C), BF16),
        compiler_params=pltpu.CompilerParams(dimension_semantics=("parallel", "parallel"),
                                             vmem_limit_bytes=VMEM_LIMIT),
        name="ssm_readout",
    )(y2, y2, xbc, zx, d_l, p['ssm_norm'].reshape(1, C))


def _merge_kernel(yf_ref, yb_ref, bv_ref, g_ref, ys_ref, gt_ref, x_ref, mod_ref, lnw_ref, lnb_ref, n1_ref, n2_ref,
                  pa_ref, pb_ref, wo_ref, rt_ref, ones_ref, x1_ref, h2_ref, aff_ref):
    C = yf_ref.shape[-1]
    ones_bd = ones_ref[...]
    y = yf_ref[...] + yb_ref[...]
    mean = _head_sum(y, ones_bd) * (1.0 / HEAD)
    yc = y - mean
    var = _head_sum(yc * yc, ones_bd) * (1.0 / HEAD)
    ya = (yc * lax.rsqrt(var + RW_GN_EPS) * lnw_ref[...] + lnb_ref[...] + bv_ref[...]) * g_ref[...]
    gt = gt_ref[...]
    mix = _sigmoid(gt[:, :C]) * _dot(ya, pa_ref[...]) + _sigmoid(gt[:, C:]) * jnp.dot(
        ys_ref[...], pb_ref[...], preferred_element_type=F32)
    ml = _dot(mix, wo_ref[...])
    rms = lambda t: t * lax.rsqrt(jnp.mean(t * t, axis=-1, keepdims=True) + NORM_EPS)
    x1 = x_ref[...] + mod_ref[2:3, :] * (rms(ml) * n1_ref[...])
    x1_ref[...] = x1
    h2 = (rms(x1) * n2_ref[...] * (1.0 + mod_ref[4:5, :]) + mod_ref[3:4, :]).astype(BF16)
    h2_ref[...] = h2
    logits = lax.dot_general(rt_ref[...], h2, (((1,), (1,)), ((), ())), preferred_element_type=F32)
    e = jnp.exp(logits - jnp.max(logits, axis=0, keepdims=True))
    aff_ref[...] = e / jnp.sum(e, axis=0, keepdims=True)


def _merge(yf, yb, bv, g, yssm, gates, x, mods, p, n1post, n2pre, router, lc, tm):
    B, S, D = x.shape
    C = yf.shape[-1]
    E = router.shape[1]
    off = lc // tm
    lat = pl.BlockSpec((None, tm, C), lambda b, i: (b, i + off, 0))
    tok = lambda w: pl.BlockSpec((None, tm, w), lambda b, i: (b, i, 0))
    full = lambda a: pl.BlockSpec(a.shape, lambda b, i: (0,) * a.ndim, pipeline_mode=pl.Buffered(1))
    row = lambda a: a.reshape(1, -1)
    params = [row(p['rw_ln_w']), row(p['rw_ln_b']), row(n1post), row(n2pre), p['proj_a'].astype(BF16),
              p['proj_b'].astype(BF16), p['w_out'].astype(BF16), jnp.transpose(router).astype(BF16), _head_ones()]
    return pl.pallas_call(
        _merge_kernel,
        grid=(B, S // tm),
        in_specs=[lat, lat, lat, lat, tok(yssm.shape[-1]), tok(2 * C), tok(D),
                  pl.BlockSpec((None, 6, D), lambda b, i: (b, 0, 0))] + [full(a) for a in params],
        out_specs=[tok(D), tok(D), pl.BlockSpec((None, E, tm), lambda b, i: (b, 0, i))],
        out_shape=[jax.ShapeDtypeStruct((B, S, D), F32), jax.ShapeDtypeStruct((B, S, D), BF16),
                   jax.ShapeDtypeStruct((B, E, S), F32)],
        compiler_params=pltpu.CompilerParams(dimension_semantics=("parallel", "parallel"),
                                             vmem_limit_bytes=VMEM_LIMIT),
        name="merge",
    )(yf, yb, bv, g, yssm, gates, x, mods, *params)


def _select_kernel(aff_ref, slot_ref, *, cap):
    aff = aff_ref[...]
    E, S = aff.shape
    bits = lax.bitcast_convert_type(aff, jnp.int32)
    count = lambda m: jnp.sum(m.astype(F32), axis=1, keepdims=True)
    thr = jnp.zeros((E, 1), jnp.int32)
    for bit in range(30, -1, -1):
        cand = thr | (1 << bit)
        thr = jnp.where(count(bits >= cand) >= cap, cand, thr)
    gt = bits > thr
    eq = bits == thr
    need = cap - count(gt)

    tl = lax.broadcasted_iota(jnp.int32, (LANES, LANES), 0)
    tc = lax.broadcasted_iota(jnp.int32, (LANES, LANES), 1)
    before = (tl < tc).astype(BF16)

    def prefix(m):
        outs, carry = [], jnp.zeros((E, 1), F32)
        for t in range(S // LANES):
            mt = m[:, t * LANES:(t + 1) * LANES]
            outs.append(jnp.dot(mt.astype(BF16), before, preferred_element_type=F32) + carry)
            carry = carry + jnp.sum(mt, axis=1, keepdims=True)
        return jnp.concatenate(outs, axis=1)

    eqf = eq.astype(F32)
    taken = gt.astype(F32) + eqf * (prefix(eqf) < need).astype(F32)
    slot_ref[...] = jnp.where(taken > 0.5, prefix(taken), -1.0).astype(jnp.int32)


def _select(aff, cap):
    B, E, S = aff.shape
    spec = pl.BlockSpec((None, E, S), lambda b: (b, 0, 0))
    return pl.pallas_call(
        functools.partial(_select_kernel, cap=cap),
        grid=(B,), in_specs=[spec], out_specs=spec,
        out_shape=jax.ShapeDtypeStruct((B, E, S), jnp.int32),
        compiler_params=pltpu.CompilerParams(dimension_semantics=("parallel",), vmem_limit_bytes=VMEM_LIMIT),
        name="ec_select",
    )(aff)


def _moe_kernel(h_ref, slot_ref, aff_ref, w1_ref, w3_ref, w2_ref, o_ref, oh_ref, *, cap, ts):
    e = pl.program_id(1)
    S, D = h_ref.shape
    slots = lax.broadcasted_iota(jnp.int32, (cap, ts), 0)
    xe = jnp.zeros((cap, D), F32)
    gate = jnp.zeros((cap, 1), F32)
    for t in range(S // ts):
        tok = slice(t * ts, (t + 1) * ts)
        hit = slots == slot_ref[pl.ds(e, 1), tok]
        gate = gate + jnp.sum(jnp.where(hit, aff_ref[pl.ds(e, 1), tok], 0.0), axis=1, keepdims=True)
        onehot = jnp.where(hit, 1.0, 0.0).astype(BF16)
        oh_ref[:, tok] = onehot
        xe = xe + jnp.dot(onehot, h_ref[tok, :], preferred_element_type=F32)
    xe = xe.astype(BF16)
    a1 = jnp.dot(xe, w1_ref[...], preferred_element_type=F32)
    a3 = jnp.dot(xe, w3_ref[...], preferred_element_type=F32)
    hid = (a1 * _sigmoid(a1)) * a3
    ye = (_dot(hid, w2_ref[...]) * gate).astype(BF16)
    for t in range(S // ts):
        tok = slice(t * ts, (t + 1) * ts)
        contrib = lax.dot_general(oh_ref[:, tok], ye, (((0,), (0,)), ((), ())), preferred_element_type=F32)

        @pl.when(e == 0)
        def _():
            o_ref[tok, :] = contrib

        @pl.when(e != 0)
        def _():
            o_ref[tok, :] += contrib


def _moe(h2, slot, aff, w1, w3, w2, cap):
    B, S, D = h2.shape
    E = slot.shape[1]
    F = w1.shape[-1]
    once = pl.Buffered(1)
    return pl.pallas_call(
        functools.partial(_moe_kernel, cap=cap, ts=min(S, 1024)),
        grid=(B, E),
        in_specs=[pl.BlockSpec((None, S, D), lambda b, e: (b, 0, 0), pipeline_mode=once),
                  pl.BlockSpec((None, E, S), lambda b, e: (b, 0, 0), pipeline_mode=once),
                  pl.BlockSpec((None, E, S), lambda b, e: (b, 0, 0), pipeline_mode=once),
                  pl.BlockSpec((None, D, F), lambda b, e: (e, 0, 0)),
                  pl.BlockSpec((None, D, F), lambda b, e: (e, 0, 0)),
                  pl.BlockSpec((None, F, D), lambda b, e: (e, 0, 0))],
        out_specs=pl.BlockSpec((None, S, D), lambda b, e: (b, 0, 0), pipeline_mode=once),
        out_shape=jax.ShapeDtypeStruct((B, S, D), F32),
        scratch_shapes=[pltpu.VMEM((cap, S), BF16)],
        compiler_params=pltpu.CompilerParams(dimension_semantics=("parallel", "arbitrary"),
                                             vmem_limit_bytes=MOE_VMEM_LIMIT),
        name="ec_moe",
    )(h2, slot, aff, w1, w3, w2)


def _final_kernel(x_ref, m_ref, g_ref, nw_ref, o_ref):
    m = m_ref[...]
    o_ref[...] = x_ref[...] + g_ref[...] * (m * lax.rsqrt(jnp.mean(m * m, axis=-1, keepdims=True) + NORM_EPS)
                                            * nw_ref[...])


def _final(x1, moe, g2, nw, tm):
    B, S, D = x1.shape
    tok = pl.BlockSpec((None, tm, D), lambda b, i: (b, i, 0))
    return pl.pallas_call(
        _final_kernel,
        grid=(B, S // tm),
        in_specs=[tok, tok, pl.BlockSpec((None, 1, D), lambda b, i: (b, 0, 0)), pl.BlockSpec((1, D), lambda b, i: (0, 0))],
        out_specs=tok,
        out_shape=jax.ShapeDtypeStruct((B, S, D), F32),
        compiler_params=pltpu.CompilerParams(dimension_semantics=("parallel", "parallel"),
                                             vmem_limit_bytes=VMEM_LIMIT),
        name="final_norm",
    )(x1, moe, g2, nw.reshape(1, D))


def _to_cm(t, rows):
    b, n, ch = t.shape
    return t.reshape(b, rows, GRID_W, ch).swapaxes(1, 2).reshape(b, n, ch)


def _from_cm(t, rows):
    b, n, ch = t.shape
    return t.reshape(b, GRID_W, rows, ch).swapaxes(1, 2).reshape(b, n, ch)


def kernel(x, c, ctx, c_ctx, ada_w, ada_b, norm1_pre, norm1_post, norm2_pre, norm2_post,
           w_in, rw_mu, rw_w0, rw_w2, rw_a0, rw_a2, rw_g2, rw_kk, rw_ka, rw_rk, rw_ln_w, rw_ln_b,
           ssm_conv_w, ssm_conv_b, ssm_dt_bias, ssm_a_log, ssm_d, ssm_norm, proj_a, proj_b, w_out,
           router, exp_w1, exp_w3, exp_w2):
    B, S, D = x.shape
    lc = ctx.shape[1]
    L = lc + S
    rows = S // GRID_W
    E = router.shape[-1]
    cap = 2 * S // E
    assert ada_w.shape[0] == 1, "single trunk layer"
    l = 0
    p = dict(rw_mu=rw_mu[l], rw_w0=rw_w0[l], rw_w2=rw_w2[l], rw_a0=rw_a0[l], rw_a2=rw_a2[l],
             rw_g2=rw_g2[l], rw_kk=rw_kk[l], rw_ka=rw_ka[l], rw_rk=rw_rk[l], rw_ln_w=rw_ln_w[l], rw_ln_b=rw_ln_b[l],
             ssm_conv_w=ssm_conv_w[l], ssm_conv_b=ssm_conv_b[l], ssm_dt_bias=ssm_dt_bias[l], ssm_a_log=ssm_a_log[l],
             ssm_d=ssm_d[l], ssm_norm=ssm_norm[l], proj_a=proj_a[l], proj_b=proj_b[l], w_out=w_out[l])
    big = lc % 256 == 0
    tt = 256 if big else 128
    chunks = 4 if big else 2
    n_rw = 3 * D + RW_LORA
    n_zx = 2 * D + (2 * D + 2 * SSM_GROUPS * SSM_STATE)
    n_dt = 2 * SSM_HEADS

    cond = jnp.concatenate([c, c_ctx[None], jnp.zeros((2 * HALO - B - 1, D), F32)], 0)
    mods_all = _adaln(cond, ada_w[l], ada_b[l], ada_w.shape[-1] // 4).reshape(2 * HALO, 6, D)
    mods, cmods = mods_all[:B], mods_all[B:B + 1]
    hl = _prenorm(x, mods, norm1_pre[l], tt)
    hc = _prenorm(ctx, cmods, norm1_pre[l], tt)
    seq_a = jnp.concatenate([hc, hl], 1).reshape(B * L, D)
    seq_b = jnp.concatenate([hc, _to_cm(hl, rows)], 1).reshape(B * L, D)
    w = w_in[l].astype(BF16)
    tm = 512 if (B * L) % 512 == 0 else 128
    proj_rw = _matmul(seq_a, w[:, :n_rw], tm, n_rw // 3).reshape(B, L, n_rw)
    gates = _matmul(hl.reshape(B * S, D), w[:, n_rw + n_zx + n_dt:], tm, D).reshape(B, S, 2 * D)
    proj_zx = _matmul(seq_b, w[:, n_rw:n_rw + n_zx], tm, D).reshape(B, L, n_zx)
    w_dt = jnp.pad(w[:, n_rw + n_zx:n_rw + n_zx + n_dt], ((0, 0), (0, LANES - n_dt)))
    proj_dt = _matmul(seq_b, w_dt, tm, LANES).reshape(B, L, LANES)[..., :n_dt]

    r, v, kk, lw, b2, ke, g, bv = _rwkv_prep(proj_rw, p, lc, tt)
    yf, ybw = _rwkv_scan(r, v, kk, lw, b2, ke, lc, chunks, 2)
    xbc = _ssm_conv(proj_zx, p['ssm_conv_w'], p['ssm_conv_b'], lc, tt, D, 2 * D)
    y2 = _ssd_scan(xbc, proj_dt, p, lc)
    yssm = _from_cm(_ssm_readout(y2, xbc, proj_zx, p, lc, tt), rows)

    x1, h2, aff = _merge(yf, ybw, bv, g, yssm, gates, x, mods, p, norm1_post[l], norm2_pre[l], router[l], lc, tt)
    slot = _select(aff, cap)
    moe = _moe(h2, slot, aff, exp_w1[l].astype(BF16), exp_w3[l].astype(BF16), exp_w2[l].astype(BF16), cap)
    return _final(x1, moe, mods[:, 5:6], norm2_post[l], tt)
```

```python
import functools
import math

import jax
import jax.numpy as jnp
from jax import lax
from jax.experimental import pallas as pl
from jax.experimental.pallas import tpu as pltpu

F32 = jnp.float32
BF16 = jnp.bfloat16

GRID_W = 64
NORM_EPS = 1e-6
RW_GN_EPS = 64e-5
HEAD = 64
LANES = 128
HALO = 16
RW_CHUNK = 64
RW_LORA = 3 * LANES
SSM_TAPS = 5
SSD_CHUNK = 128
SSM_HEADS, SSM_GROUPS, SSM_STATE = 32, 4, 128
VMEM_LIMIT = 48 * 1024 * 1024
MOE_VMEM_LIMIT = 56 * 1024 * 1024


def _dot(a, b):
    return jnp.dot(a.astype(BF16), b.astype(BF16), preferred_element_type=F32)


def _dot_nt(a, b):
    return lax.dot_general(a.astype(BF16), b.astype(BF16), (((1,), (1,)), ((), ())), preferred_element_type=F32)


def _split3(x):
    hi = x.astype(BF16)
    r1 = x - hi.astype(F32)
    mid = r1.astype(BF16)
    lo = (r1 - mid.astype(F32)).astype(BF16)
    return hi, mid, lo


def _dot_exact_lhs(m, x):
    mb = m.astype(BF16)
    hi, mid, lo = _split3(x)
    return (jnp.dot(mb, hi, preferred_element_type=F32) + jnp.dot(mb, mid, preferred_element_type=F32)
            + jnp.dot(mb, lo, preferred_element_type=F32))


def _dot3(x, m):
    mb = m.astype(BF16)
    hi, mid, lo = _split3(x)
    return (jnp.dot(hi, mb, preferred_element_type=F32) + jnp.dot(mid, mb, preferred_element_type=F32)
            + jnp.dot(lo, mb, preferred_element_type=F32))


def _sigmoid(x):
    return 1.0 / (1.0 + jnp.exp(-x))


def _softplus(x):
    return jnp.maximum(x, 0.0) + jnp.log(1.0 + jnp.exp(-jnp.abs(x)))


def _head_sum(x, ones_bd):
    n = x.shape[1] // LANES
    return jnp.concatenate([_dot3(x[:, i * LANES:(i + 1) * LANES], ones_bd) for i in range(n)], axis=1)


def _head_ones():
    i = jnp.arange(LANES)
    return ((i[:, None] // HEAD) == (i[None, :] // HEAD)).astype(F32)


def _halo_specs(tt, width, col, L):
    per = tt // HALO
    nh = L // HALO
    main = pl.BlockSpec((None, tt, width), lambda b, i, *_: (b, i, col(*_)))
    prev = pl.BlockSpec((None, HALO, width), lambda b, i, *_: (b, jnp.maximum(i * per - 1, 0), col(*_)))
    nxt = pl.BlockSpec((None, HALO, width), lambda b, i, *_: (b, jnp.minimum((i + 1) * per, nh - 1), col(*_)))
    return main, prev, nxt


def _seg_edges(i, tt, lc, L):
    t0 = i * tt
    has_prev = jnp.logical_and(t0 != 0, t0 != lc)
    has_next = jnp.logical_and(t0 + tt != lc, t0 + tt != L)
    return has_prev.astype(F32), has_next.astype(F32)


def _adaln_kernel(c_ref, w_ref, b_ref, o_ref):
    c = c_ref[...]
    o_ref[...] = jnp.dot(c * _sigmoid(c), w_ref[...], preferred_element_type=F32,
                         precision=lax.Precision.HIGHEST) + b_ref[...]


def _adaln(cond, w, b, tn):
    M, D = cond.shape
    N = w.shape[1]
    return pl.pallas_call(
        _adaln_kernel,
        grid=(N // tn,),
        in_specs=[pl.BlockSpec((M, D), lambda j: (0, 0)), pl.BlockSpec((D, tn), lambda j: (0, j)),
                  pl.BlockSpec((1, tn), lambda j: (0, j))],
        out_specs=pl.BlockSpec((M, tn), lambda j: (0, j)),
        out_shape=jax.ShapeDtypeStruct((M, N), F32),
        compiler_params=pltpu.CompilerParams(dimension_semantics=("parallel",), vmem_limit_bytes=VMEM_LIMIT),
        name="adaln",
    )(cond, w, b.reshape(1, N))


def _prenorm_kernel(x_ref, mod_ref, nw_ref, o_ref):
    x = x_ref[...]
    y = x * lax.rsqrt(jnp.mean(x * x, axis=-1, keepdims=True) + NORM_EPS) * nw_ref[...]
    o_ref[...] = (y * (1.0 + mod_ref[1:2, :]) + mod_ref[0:1, :]).astype(o_ref.dtype)


def _prenorm(x, mods, nw, tm):
    B, N, D = x.shape
    per_sample = mods.shape[0] == B
    tok = pl.BlockSpec((None, tm, D), lambda b, i: (b, i, 0))
    return pl.pallas_call(
        _prenorm_kernel,
        grid=(B, N // tm),
        in_specs=[tok, pl.BlockSpec((None, 6, D), (lambda b, i: (b, 0, 0)) if per_sample else (lambda b, i: (0, 0, 0))),
                  pl.BlockSpec((1, D), lambda b, i: (0, 0))],
        out_specs=tok,
        out_shape=jax.ShapeDtypeStruct((B, N, D), BF16),
        compiler_params=pltpu.CompilerParams(dimension_semantics=("parallel", "parallel"),
                                             vmem_limit_bytes=VMEM_LIMIT),
        name="prenorm",
    )(x, mods, nw.reshape(1, D))


def _matmul_kernel(a_ref, w_ref, o_ref):
    o_ref[...] = jnp.dot(a_ref[...], w_ref[...], preferred_element_type=F32).astype(o_ref.dtype)


def _matmul(a, w, tm, tn, out_dtype=F32):
    M, K = a.shape
    N = w.shape[1]
    assert M % tm == 0 and N % tn == 0
    return pl.pallas_call(
        _matmul_kernel,
        grid=(M // tm, N // tn),
        in_specs=[pl.BlockSpec((tm, K), lambda i, j: (i, 0)), pl.BlockSpec((K, tn), lambda i, j: (0, j))],
        out_specs=pl.BlockSpec((tm, tn), lambda i, j: (i, j)),
        out_shape=jax.ShapeDtypeStruct((M, N), out_dtype),
        compiler_params=pltpu.CompilerParams(dimension_semantics=("parallel", "parallel"),
                                             vmem_limit_bytes=VMEM_LIMIT),
        name="matmul",
    )(a, w)


def _rwkv_prep_kernel(p_ref, pp_ref, pn_ref, mu_ref, w2_ref, a2_ref, g2_ref, w0_ref, a0_ref, kkw_ref, ka_ref,
                      rk_ref, ones_ref, r_ref, v_ref, kk_ref, lw_ref, b_ref, ke_ref, g_ref, bv_ref, *, tt, lc, L):
    has_prev, has_next = _seg_edges(pl.program_id(1), tt, lc, L)
    rows = lax.broadcasted_iota(jnp.int32, (tt, 1), 0)
    first, last = rows == 0, rows == tt - 1

    def shifted(lo, hi):
        u = p_ref[:, lo:hi].astype(F32)
        pv = pp_ref[HALO - 1:HALO, lo:hi].astype(F32) * has_prev
        nx = pn_ref[0:1, lo:hi].astype(F32) * has_next
        up = jnp.where(first, pv, pltpu.roll(u, 1, 0))
        un = jnp.where(last, nx, pltpu.roll(u, tt - 1, 0))
        return u + mu_ref[0:1, lo:hi] * (up - u) + mu_ref[1:2, lo:hi] * (un - u)

    C = r_ref.shape[-1]
    lane = lax.broadcasted_iota(jnp.int32, (1, LANES), 1)
    half = [(lane < HEAD).astype(F32), (lane >= HEAD).astype(F32)]
    xw = jnp.tanh(shifted(3 * C, 3 * C + LANES))
    xa = shifted(3 * C + LANES, 3 * C + 2 * LANES)
    xg = _sigmoid(shifted(3 * C + 2 * LANES, 3 * C + 3 * LANES))
    g_ref[...] = _dot(xg, g2_ref[...]).astype(g_ref.dtype)
    r = shifted(0, C)
    k = shifted(C, 2 * C)
    v = shifted(2 * C, 3 * C)
    r_ref[...] = r.astype(r_ref.dtype)
    v_ref[...] = v.astype(v_ref.dtype)
    ones_bd = ones_ref[...]
    kkr = k * kkw_ref[...]
    kk = kkr * lax.rsqrt(jnp.maximum(_head_sum(kkr * kkr, ones_bd), 1e-12))
    kk_ref[...] = kk.astype(kk_ref.dtype)
    ksum = jnp.zeros_like(k)
    for d in (0, 1):
        w_raw = w0_ref[d:d + 1, :] + _dot(xw * half[d], w2_ref[...])
        lw_ref[d] = -math.exp(-0.5) * _sigmoid(w_raw)
        a = _sigmoid(a0_ref[d:d + 1, :] + _dot(xa * half[d], a2_ref[...]))
        b_ref[d] = (kk * a).astype(b_ref.dtype)
        ke = k * (1.0 + (a - 1.0) * ka_ref[...])
        ke_ref[d] = ke.astype(ke_ref.dtype)
        ksum = ksum + ke
    bv_ref[...] = (_head_sum(r * ksum * rk_ref[...], ones_bd) * v).astype(bv_ref.dtype)


def _rwkv_prep(proj, p, lc, tt):
    B, L, W = proj.shape
    C = (W - RW_LORA) // 3
    assert L % tt == 0 and lc % tt == 0
    main, prev, nxt = _halo_specs(tt, W, lambda: 0, L)
    full = lambda a: pl.BlockSpec(a.shape, lambda b, i: (0,) * a.ndim)
    w2 = p['rw_w2'].reshape(2 * HEAD, C)
    a2 = p['rw_a2'].reshape(2 * HEAD, C)
    params = [p['rw_mu'], w2, a2, p['rw_g2'], p['rw_w0'], p['rw_a0'], p['rw_kk'].reshape(1, C),
              p['rw_ka'].reshape(1, C), p['rw_rk'].reshape(1, C), _head_ones()]
    one = pl.BlockSpec((None, tt, C), lambda b, i: (b, i, 0))
    two = pl.BlockSpec((2, None, tt, C), lambda b, i: (0, b, i, 0))
    s1 = jax.ShapeDtypeStruct((B, L, C), BF16)
    s2 = jax.ShapeDtypeStruct((2, B, L, C), BF16)
    return pl.pallas_call(
        functools.partial(_rwkv_prep_kernel, tt=tt, lc=lc, L=L),
        grid=(B, L // tt),
        in_specs=[main, prev, nxt] + [full(a) for a in params],
        out_specs=[one, one, one, two, two, two, one, one],
        out_shape=[s1, s1, s1, jax.ShapeDtypeStruct((2, B, L, C), F32), s2, s2, s1, s1],
        compiler_params=pltpu.CompilerParams(dimension_semantics=("parallel", "parallel"),
                                             vmem_limit_bytes=VMEM_LIMIT),
        name="rwkv_prep",
    )(proj, proj, proj, *params)


def _rwkv_masks():
    T = RW_CHUNK
    idx = jnp.arange(2 * T)
    same = (idx[:, None] // T) == (idx[None, :] // T)
    out = []
    for rev in (False, True):
        t = idx % T
        t = (T - 1 - t) if rev else t
        rt, ct = t[:, None], t[None, :]
        ms = [same & (rt > ct), same & (rt >= ct)]
        s = 1
        while s < T:
            br, bc = rt // s, ct // s
            ms.append(same & (br // 2 == bc // 2) & (br % 2 == 1) & (bc % 2 == 0))
            s *= 2
        out.append(jnp.stack(ms))
    return jnp.stack(out).astype(F32)


def _rwkv_scan_kernel(m_ref, rf_ref, vf_ref, kkf_ref, rb_ref, vb_ref, kkb_ref, lwf_ref, bf_ref, kf_ref,
                      lwb_ref, bb_ref, kb_ref, yf_ref, yb_ref, st_ref, *, chunks, pairs):
    T = RW_CHUNK

    @pl.when(pl.program_id(2) == 0)
    def _():
        st_ref[...] = jnp.zeros_like(st_ref)

    row = lax.broadcasted_iota(jnp.int32, (2 * T, 2 * T), 0)
    col = lax.broadcasted_iota(jnp.int32, (2 * T, 2 * T), 1)
    eye = (row == col).astype(F32)
    bd = ((row >> 6) == (col >> 6)).astype(F32)
    head0 = lax.broadcasted_iota(jnp.int32, (T, LANES), 1) < HEAD
    h0f = head0.astype(F32)
    h1f = 1.0 - h0f
    refs = ((rf_ref, vf_ref, kkf_ref, lwf_ref, bf_ref, kf_ref, yf_ref),
            (rb_ref, vb_ref, kkb_ref, lwb_ref, bb_ref, kb_ref, yb_ref))
    chains = [(d, q) for d in (0, 1) for q in range(pairs)]
    order = {0: list(range(chunks)), 1: list(range(chunks - 1, -1, -1))}
    units = [(d, q, j) for j in range(chunks) for (d, q) in chains]
    n_levels = RW_CHUNK.bit_length() - 1

    def tile(ref, d, q, j):
        cj = order[d][j]
        return ref[cj * T:(cj + 1) * T, q * LANES:(q + 1) * LANES].astype(F32)

    A = {}
    for un in units:
        d, q, j = un
        r_ref, v_ref, kk_ref, lw_ref, b_ref, k_ref, _ = refs[d]
        lw = tile(lw_ref, d, q, j)
        c = _dot_exact_lhs(m_ref[d, 1, :T, :T], lw)
        gl = jnp.exp(jnp.sum(lw, axis=0, keepdims=True))
        e_nc = jnp.exp(-c)
        at = -tile(kk_ref, d, q, j) * jnp.exp(c - lw)
        rtl = tile(r_ref, d, q, j) * jnp.exp(c)
        bt = tile(b_ref, d, q, j) * e_nc
        kt = tile(k_ref, d, q, j) * e_nc
        A[un] = dict(gl=gl, at=at, rtl=rtl, bt=bt, kt=kt)
    for un in units:
        a = A[un]
        lhs = jnp.concatenate([a['at'] * h0f, a['at'] * h1f, a['rtl'] * h0f, a['rtl'] * h1f], axis=0)
        rhs = jnp.concatenate([a['bt'], a['bt'], a['kt'], a['kt']], axis=0)
        a['aa'] = _dot_nt(lhs, rhs)
    for un in units:
        d = un[0]
        a = A[un]
        aa = a.pop('aa')
        a['n'] = aa[:2 * T, :2 * T] * m_ref[d, 0]
        a['ak'] = (aa[:2 * T, 2 * T:] * m_ref[d, 0]).astype(BF16)
        a['rbrk'] = jnp.concatenate([aa[2 * T:, :2 * T] * m_ref[d, 1], aa[2 * T:, 2 * T:] * m_ref[d, 1]],
                                    axis=1).astype(BF16)
        a['p'] = eye + a['n'] * m_ref[d, 2]
    for lv in range(1, n_levels):
        for un in units:
            a = A[un]
            a['pn'] = _dot(a['p'], a['n'] * m_ref[un[0], 2 + lv])
        for un in units:
            a = A[un]
            a['p'] = a['p'] + _dot(a.pop('pn'), a['p'])
    for un in units:
        d, q, j = un
        a = A[un]
        v = tile(refs[d][1], d, q, j)
        a['akv'] = _dot(a.pop('ak'), jnp.concatenate([v, v], axis=0))
        a['p'] = a['p'].astype(BF16)
        a['ar'] = jnp.concatenate([a.pop('at'), a.pop('rtl')], axis=0).astype(BF16)
        a['bk'] = jnp.concatenate([a.pop('bt') * a['gl'], a.pop('kt') * a['gl']], axis=0).astype(BF16)
        a.pop('n')

    st = {ch: st_ref[ci] for ci, ch in enumerate(chains)}
    for j in range(chunks):
        x = {}
        for ch in chains:
            x[ch] = _dot_nt(A[ch + (j,)]['ar'], st[ch])
        u_st = {}
        for ch in chains:
            a = A[ch + (j,)]
            xa = x[ch][:T]
            u_st[ch] = _dot(a['p'], jnp.concatenate([xa, xa], axis=0) + a['akv'])
        for ch in chains:
            d, q = ch
            a = A[ch + (j,)]
            v = tile(refs[d][1], d, q, j)
            y_st = _dot(a['rbrk'], jnp.concatenate([u_st[ch], jnp.concatenate([v, v], axis=0)], axis=0))
            cj = order[d][j]
            refs[d][6][cj * T:(cj + 1) * T, q * LANES:(q + 1) * LANES] = (
                jnp.where(head0, y_st[:T], y_st[T:]) + x[ch][T:]).astype(yf_ref.dtype)
            u = jnp.where(head0, u_st[ch][:T], u_st[ch][T:])
            upd = _dot(jnp.concatenate([u, v], axis=0).T, a['bk'])
            st[ch] = (st[ch] * a['gl'] + upd) * bd
    for ci, ch in enumerate(chains):
        st_ref[ci] = st[ch]


def _rwkv_scan(r, v, kk, lw, b2, k2, lc, chunks, pairs):
    B, L, C = r.shape
    tb = chunks * RW_CHUNK
    wl = pairs * LANES
    assert L % tb == 0 and lc % tb == 0 and C % wl == 0
    nblk, nctx = L // tb, lc // tb

    def bwd(i):
        return jnp.where(i < nctx, nctx - 1 - i, nblk - 1 - (i - nctx))

    sf = pl.BlockSpec((None, tb, wl), lambda b, p, i: (b, i, p))
    sb = pl.BlockSpec((None, tb, wl), lambda b, p, i: (b, bwd(i), p))
    df = pl.BlockSpec((None, None, tb, wl), lambda b, p, i: (0, b, i, p))
    db = pl.BlockSpec((None, None, tb, wl), lambda b, p, i: (1, b, bwd(i), p))
    masks = _rwkv_masks()
    return pl.pallas_call(
        functools.partial(_rwkv_scan_kernel, chunks=chunks, pairs=pairs),
        grid=(B, C // wl, nblk),
        in_specs=[pl.BlockSpec(masks.shape, lambda b, p, i: (0, 0, 0, 0)),
                  sf, sf, sf, sb, sb, sb, df, df, df, db, db, db],
        out_specs=[sf, sb],
        out_shape=[jax.ShapeDtypeStruct((B, L, C), BF16)] * 2,
        scratch_shapes=[pltpu.VMEM((2 * pairs, LANES, LANES), F32)],
        compiler_params=pltpu.CompilerParams(
            dimension_semantics=("parallel", "parallel", "arbitrary"),
            vmem_limit_bytes=VMEM_LIMIT),
        name="rwkv_scan",
    )(masks, r, v, kk, r, v, kk, lw, b2, k2, lw, b2, k2)


def _ssm_conv_kernel(x_ref, xp_ref, xn_ref, w_ref, b_ref, o_ref, *, tt, lc, L):
    has_prev, has_next = _seg_edges(pl.program_id(1), tt, lc, L)
    ext = jnp.concatenate([xp_ref[...].astype(F32) * has_prev, x_ref[...].astype(F32),
                           xn_ref[...].astype(F32) * has_next], axis=0)
    n = tt + 2 * HALO
    acc = jnp.zeros(x_ref.shape, F32) + b_ref[...]
    for j in range(SSM_TAPS):
        off = j - SSM_TAPS // 2
        sh = ext if off == 0 else pltpu.roll(ext, (-off) % n, 0)
        acc = acc + sh[HALO:HALO + tt] * w_ref[j:j + 1, :]
    o_ref[...] = (acc * _sigmoid(acc)).astype(o_ref.dtype)


def _ssm_conv(zx, w, b, lc, tt, tw, col0):
    B, L, _ = zx.shape
    cc = w.shape[0]
    assert cc % tw == 0 and col0 % tw == 0 and L % tt == 0 and lc % tt == 0
    main, prev, nxt = _halo_specs(tt, tw, lambda c: c + col0 // tw, L)
    wt = jnp.transpose(w)
    return pl.pallas_call(
        functools.partial(_ssm_conv_kernel, tt=tt, lc=lc, L=L),
        grid=(B, L // tt, cc // tw),
        in_specs=[main, prev, nxt, pl.BlockSpec((SSM_TAPS, tw), lambda b, i, c: (0, c)),
                  pl.BlockSpec((1, tw), lambda b, i, c: (0, c))],
        out_specs=pl.BlockSpec((None, tt, tw), lambda b, i, c: (b, i, c)),
        out_shape=jax.ShapeDtypeStruct((B, L, cc), BF16),
        compiler_params=pltpu.CompilerParams(dimension_semantics=("parallel", "parallel", "parallel"),
                                             vmem_limit_bytes=VMEM_LIMIT),
        name="ssm_conv",
    )(zx, zx, zx, wt, b.reshape(1, cc))


def _ssd_scan_kernel(x_ref, bm_ref, cm_ref, dt_ref, dtt_ref, a_ref, at_ref, bias_ref, biast_ref, y_ref,
                     ht_ref, *, groups, hpg):
    Q = SSD_CHUNK
    rev = pl.program_id(1) == 1

    @pl.when(pl.program_id(2) == 0)
    def _():
        ht_ref[...] = jnp.zeros_like(ht_ref)

    row = lax.broadcasted_iota(jnp.int32, (Q, Q), 0)
    col = lax.broadcasted_iota(jnp.int32, (Q, Q), 1)
    incl = jnp.where(rev, col - row, row - col) >= 0
    tri = incl.astype(F32)
    dt = _softplus(dt_ref[...] + bias_ref[...])
    cum = _dot_exact_lhs(tri, dt * a_ref[...])
    dtt = _softplus(dtt_ref[...] + biast_ref[...])
    hi, mid, lo = _split3(dtt * at_ref[...])
    trib = tri.astype(BF16)
    nt = lambda a_, b_: lax.dot_general(a_, b_, (((1,), (1,)), ((), ())), preferred_element_type=F32)
    cumt = nt(hi, trib) + nt(mid, trib) + nt(lo, trib)
    head0 = lax.broadcasted_iota(jnp.int32, (1, LANES), 1) < HEAD
    h0f = head0.astype(F32)
    h1f = 1.0 - h0f
    for g in range(groups):
        bm = bm_ref[:, g * SSM_STATE:(g + 1) * SSM_STATE]
        cm = cm_ref[:, g * SSM_STATE:(g + 1) * SSM_STATE]
        cb = _dot_nt(cm, bm)
        bmt = bm.astype(F32).T
        for pp in range(hpg // 2):
            e0 = g * hpg + 2 * pp
            lanes = slice((e0 // 2) * LANES, (e0 // 2 + 1) * LANES)
            cum_l = jnp.where(head0, cum[:, e0:e0 + 1], cum[:, e0 + 1:e0 + 2])
            dt_l = jnp.where(head0, dt[:, e0:e0 + 1], dt[:, e0 + 1:e0 + 2])
            ms = []
            for e in (e0, e0 + 1):
                seg = jnp.exp(jnp.where(incl, cum[:, e:e + 1] - cumt[e:e + 1, :], -jnp.inf))
                ms.append(cb * seg)
            xdt = x_ref[:, lanes].astype(F32) * dt_l
            clast = jnp.min(cum_l, axis=0, keepdims=True)
            htp = ht_ref[:, lanes]
            y = _dot(jnp.concatenate(ms, axis=1), jnp.concatenate([xdt * h0f, xdt * h1f], axis=0))
            y_ref[:, lanes] = (y + _dot(cm, htp) * jnp.exp(cum_l)).astype(y_ref.dtype)
            ht_ref[:, lanes] = htp * jnp.exp(clast) + _dot(bmt, xdt * jnp.exp(clast - cum_l))


def _ssd_scan(xbc, dtp, p, lc):
    B, L, _ = xbc.shape
    Q = SSD_CHUNK
    H, G, N = SSM_HEADS, SSM_GROUPS, SSM_STATE
    xw = H * HEAD
    gw = G * N
    assert L % Q == 0 and lc % Q == 0 and xw % gw == 0
    nck, nctx = L // Q, lc // Q
    dt = jnp.pad(jnp.moveaxis(dtp.reshape(B, L, 2, H), 2, 1), ((0, 0), (0, 0), (0, 0), (0, LANES - H)))
    dtt = jnp.swapaxes(dt, 2, 3)
    a = jnp.pad(-jnp.exp(p['ssm_a_log']), ((0, 0), (0, LANES - H)))
    bias = jnp.pad(p['ssm_dt_bias'], ((0, 0), (0, LANES - H)))

    def ck(d, i):
        bwd = jnp.where(i < nctx, nctx - 1 - i, nck - 1 - (i - nctx))
        return jnp.where(d == 0, i, bwd)

    return pl.pallas_call(
        functools.partial(_ssd_scan_kernel, groups=G, hpg=H // G),
        grid=(B, 2, nck),
        in_specs=[pl.BlockSpec((None, Q, xw), lambda b, d, i: (b, ck(d, i), 0)),
                  pl.BlockSpec((None, Q, gw), lambda b, d, i: (b, ck(d, i), xw // gw)),
                  pl.BlockSpec((None, Q, gw), lambda b, d, i: (b, ck(d, i), xw // gw + 1)),
                  pl.BlockSpec((None, None, Q, LANES), lambda b, d, i: (b, d, ck(d, i), 0)),
                  pl.BlockSpec((None, None, LANES, Q), lambda b, d, i: (b, d, 0, ck(d, i))),
                  pl.BlockSpec((None, 1, LANES), lambda b, d, i: (d, 0, 0)),
                  pl.BlockSpec((None, LANES, 1), lambda b, d, i: (d, 0, 0)),
                  pl.BlockSpec((None, 1, LANES), lambda b, d, i: (d, 0, 0)),
                  pl.BlockSpec((None, LANES, 1), lambda b, d, i: (d, 0, 0))],
        out_specs=pl.BlockSpec((None, None, Q, xw), lambda b, d, i: (d, b, ck(d, i), 0)),
        out_shape=jax.ShapeDtypeStruct((2, B, L, xw), BF16),
        scratch_shapes=[pltpu.VMEM((N, xw), F32)],
        compiler_params=pltpu.CompilerParams(
            dimension_semantics=("parallel", "arbitrary", "arbitrary"),
            vmem_limit_bytes=VMEM_LIMIT),
        name="ssd_scan",
    )(xbc, xbc, xbc, dt, dtt, a.reshape(2, 1, LANES), a.reshape(2, LANES, 1), bias.reshape(2, 1, LANES),
      bias.reshape(2, LANES, 1))


def _ssm_readout_kernel(yf_ref, yb_ref, x_ref, z_ref, d_ref, nw_ref, o_ref):
    z = z_ref[...].astype(F32)
    y = (yf_ref[...].astype(F32) + yb_ref[...].astype(F32) + d_ref[...] * x_ref[...].astype(F32)) * (z * _sigmoid(z))
    o_ref[...] = (y * lax.rsqrt(jnp.mean(y * y, axis=-1, keepdims=True) + NORM_EPS) * nw_ref[...]).astype(o_ref.dtype)


def _ssm_readout(y2, xbc, zx, p, lc, tt):
    _, B, L, C = y2.shape
    S = L - lc
    off = lc // tt
    d_l = jnp.repeat(p['ssm_d'], HEAD).reshape(1, C)
    row = lambda b, i: (b, i + off, 0)
    return pl.pallas_call(
        _ssm_readout_kernel,
        grid=(B, S // tt),
        in_specs=[pl.BlockSpec((None, None, tt, C), lambda b, i: (0, b, i + off, 0)),
                  pl.BlockSpec((None, None, tt, C), lambda b, i: (1, b, i + off, 0)),
                  pl.BlockSpec((None, tt, C), row), pl.BlockSpec((None, tt, C), row),
                  pl.BlockSpec((1, C), lambda b, i: (0, 0)), pl.BlockSpec((1, C), lambda b, i: (0, 0))],
        out_specs=pl.BlockSpec((None, tt, C), lambda b, i: (b, i, 0)),
        out_shape=jax.ShapeDtypeStruct((B, S, C), BF16),
        compiler_params=pltpu.CompilerParams(dimension_semantics=("parallel", "parallel"),
                                             vmem_limit_bytes=VMEM_LIMIT),
        name="ssm_readout",
    )(y2, y2, xbc, zx, d_l, p['ssm_norm'].reshape(1, C))


def _merge_kernel(yf_ref, yb_ref, bv_ref, g_ref, ys_ref, gt_ref, x_ref, mod_ref, lnw_ref, lnb_ref, n1_ref, n2_ref,
                  pa_ref, pb_ref, wo_ref, rt_ref, ones_ref, x1_ref, h2_ref, aff_ref):
    C = yf_ref.shape[-1]
    ones_bd = ones_ref[...]
    y = yf_ref[...].astype(F32) + yb_ref[...].astype(F32)
    mean = _head_sum(y, ones_bd) * (1.0 / HEAD)
    yc = y - mean
    var = _head_sum(yc * yc, ones_bd) * (1.0 / HEAD)
    ya = (yc * lax.rsqrt(var + RW_GN_EPS) * lnw_ref[...] + lnb_ref[...] + bv_ref[...].astype(F32)) * g_ref[...].astype(F32)
    gt = gt_ref[...].astype(F32)
    mix = _sigmoid(gt[:, :C]) * _dot(ya, pa_ref[...]) + _sigmoid(gt[:, C:]) * jnp.dot(
        ys_ref[...], pb_ref[...], preferred_element_type=F32)
    ml = _dot(mix, wo_ref[...])
    rms = lambda t: t * lax.rsqrt(jnp.mean(t * t, axis=-1, keepdims=True) + NORM_EPS)
    x1 = x_ref[...] + mod_ref[2:3, :] * (rms(ml) * n1_ref[...])
    x1_ref[...] = x1
    h2 = (rms(x1) * n2_ref[...] * (1.0 + mod_ref[4:5, :]) + mod_ref[3:4, :]).astype(BF16)
    h2_ref[...] = h2
    logits = lax.dot_general(rt_ref[...], h2, (((1,), (1,)), ((), ())), preferred_element_type=F32)
    e = jnp.exp(logits - jnp.max(logits, axis=0, keepdims=True))
    aff_ref[...] = e / jnp.sum(e, axis=0, keepdims=True)


def _merge(yf, yb, bv, g, yssm, gates, x, mods, p, n1post, n2pre, router, lc, tm):
    B, S, D = x.shape
    C = yf.shape[-1]
    E = router.shape[1]
    off = lc // tm
    lat = pl.BlockSpec((None, tm, C), lambda b, i: (b, i + off, 0))
    tok = lambda w: pl.BlockSpec((None, tm, w), lambda b, i: (b, i, 0))
    full = lambda a: pl.BlockSpec(a.shape, lambda b, i: (0,) * a.ndim, pipeline_mode=pl.Buffered(1))
    row = lambda a: a.reshape(1, -1)
    params = [row(p['rw_ln_w']), row(p['rw_ln_b']), row(n1post), row(n2pre), p['proj_a'].astype(BF16),
              p['proj_b'].astype(BF16), p['w_out'].astype(BF16), jnp.transpose(router).astype(BF16), _head_ones()]
    return pl.pallas_call(
        _merge_kernel,
        grid=(B, S // tm),
        in_specs=[lat, lat, lat, lat, tok(yssm.shape[-1]), tok(2 * C), tok(D),
                  pl.BlockSpec((None, 6, D), lambda b, i: (b, 0, 0))] + [full(a) for a in params],
        out_specs=[tok(D), tok(D), pl.BlockSpec((None, E, tm), lambda b, i: (b, 0, i))],
        out_shape=[jax.ShapeDtypeStruct((B, S, D), F32), jax.ShapeDtypeStruct((B, S, D), BF16),
                   jax.ShapeDtypeStruct((B, E, S), F32)],
        compiler_params=pltpu.CompilerParams(dimension_semantics=("parallel", "parallel"),
                                             vmem_limit_bytes=VMEM_LIMIT),
        name="merge",
    )(yf, yb, bv, g, yssm, gates, x, mods, *params)


def _select_kernel(aff_ref, slot_ref, off_ref, *, cap):
    aff = aff_ref[...]
    E, S = aff.shape
    bits = lax.bitcast_convert_type(aff, jnp.int32)
    count = lambda m: jnp.sum(m.astype(F32), axis=1, keepdims=True)
    thr = jnp.zeros((E, 1), jnp.int32)
    for bit in range(30, -1, -1):
        cand = thr | (1 << bit)
        thr = jnp.where(count(bits >= cand) >= cap, cand, thr)
    gt = bits > thr
    eq = bits == thr
    need = cap - count(gt)

    tl = lax.broadcasted_iota(jnp.int32, (LANES, LANES), 0)
    tc = lax.broadcasted_iota(jnp.int32, (LANES, LANES), 1)
    before = (tl < tc).astype(BF16)

    def prefix(m):
        outs, carry = [], jnp.zeros((E, 1), F32)
        for t in range(S // LANES):
            mt = m[:, t * LANES:(t + 1) * LANES]
            outs.append(jnp.dot(mt.astype(BF16), before, preferred_element_type=F32) + carry)
            carry = carry + jnp.sum(mt, axis=1, keepdims=True)
        return jnp.concatenate(outs, axis=1)

    eqf = eq.astype(F32)
    taken = gt.astype(F32) + eqf * (prefix(eqf) < need).astype(F32)
    slot_ref[...] = jnp.where(taken > 0.5, prefix(taken), -1.0).astype(jnp.int32)
    tok = lax.broadcasted_iota(jnp.int32, (S, LANES), 0)
    tile_start = lax.broadcasted_iota(jnp.int32, (S, LANES), 1) * LANES
    off_ref[...] = jnp.dot(taken.astype(BF16), (tok < tile_start).astype(BF16),
                           preferred_element_type=F32).astype(jnp.int32)


def _select(aff, cap):
    B, E, S = aff.shape
    spec = pl.BlockSpec((None, E, S), lambda b: (b, 0, 0))
    return pl.pallas_call(
        functools.partial(_select_kernel, cap=cap),
        grid=(B,), in_specs=[spec], out_specs=[spec, pl.BlockSpec((None, E, LANES), lambda b: (b, 0, 0))],
        out_shape=[jax.ShapeDtypeStruct((B, E, S), jnp.int32), jax.ShapeDtypeStruct((B, E, LANES), jnp.int32)],
        compiler_params=pltpu.CompilerParams(dimension_semantics=("parallel",), vmem_limit_bytes=VMEM_LIMIT),
        name="ec_select",
    )(aff)


def _moe_kernel(off_ref, h_ref, slot_ref, aff_ref, w1_ref, w3_ref, w2_ref, o_ref, xe_ref, gate_ref, ye_ref,
                *, cap, ts, sb):
    b, e = pl.program_id(0), pl.program_id(1)
    S, D = h_ref.shape
    nt, nsb = S // ts, cap // sb
    base = (b * pl.num_programs(1) + e) * (nt + 1)
    slot_iota = lax.broadcasted_iota(jnp.int32, (sb, ts), 0)

    def pairs(body):
        for t in range(nt):
            lo, hi = off_ref[base + t], off_ref[base + t + 1]
            for k in range(nsb):
                def run(t=t, k=k):
                    tok = slice(t * ts, (t + 1) * ts)
                    hit = (slot_iota + k * sb) == slot_ref[pl.ds(e, 1), tok]
                    body(hit, tok, slice(k * sb, (k + 1) * sb))
                pl.when(jnp.logical_and(lo < (k + 1) * sb, hi > k * sb))(run)

    xe_ref[...] = jnp.zeros_like(xe_ref)
    gate_ref[...] = jnp.zeros_like(gate_ref)

    @pl.when(e == 0)
    def _():
        o_ref[...] = jnp.zeros_like(o_ref)

    def gather(hit, tok, rows):
        gate_ref[rows, :] += jnp.sum(jnp.where(hit, aff_ref[pl.ds(e, 1), tok], 0.0), axis=1, keepdims=True)
        xe_ref[rows, :] += jnp.dot(jnp.where(hit, 1.0, 0.0).astype(BF16), h_ref[tok, :],
                                   preferred_element_type=F32)

    pairs(gather)
    xe = xe_ref[...].astype(BF16)
    a1 = jnp.dot(xe, w1_ref[...], preferred_element_type=F32)
    a3 = jnp.dot(xe, w3_ref[...], preferred_element_type=F32)
    hid = (a1 * _sigmoid(a1)) * a3
    ye_ref[...] = (_dot(hid, w2_ref[...]) * gate_ref[...]).astype(BF16)

    def scatter(hit, tok, rows):
        o_ref[tok, :] += lax.dot_general(jnp.where(hit, 1.0, 0.0).astype(BF16), ye_ref[rows, :],
                                         (((0,), (0,)), ((), ())), preferred_element_type=F32)

    pairs(scatter)


def _moe(h2, slot, offs, aff, w1, w3, w2, cap):
    B, S, D = h2.shape
    E = slot.shape[1]
    F = w1.shape[-1]
    ts, sb = min(S, 512), min(cap, LANES)
    nt = S // ts
    bounds = jnp.concatenate([offs[..., ::ts // LANES][..., :nt], jnp.full((B, E, 1), cap, jnp.int32)], -1).reshape(-1)
    once = pl.Buffered(1)
    return pl.pallas_call(
        functools.partial(_moe_kernel, cap=cap, ts=ts, sb=sb),
        grid_spec=pltpu.PrefetchScalarGridSpec(
            num_scalar_prefetch=1,
            grid=(B, E),
            in_specs=[pl.BlockSpec((None, S, D), lambda b, e, o: (b, 0, 0), pipeline_mode=once),
                      pl.BlockSpec((None, E, S), lambda b, e, o: (b, 0, 0), pipeline_mode=once),
                      pl.BlockSpec((None, E, S), lambda b, e, o: (b, 0, 0), pipeline_mode=once),
                      pl.BlockSpec((None, D, F), lambda b, e, o: (e, 0, 0)),
                      pl.BlockSpec((None, D, F), lambda b, e, o: (e, 0, 0)),
                      pl.BlockSpec((None, F, D), lambda b, e, o: (e, 0, 0))],
            out_specs=pl.BlockSpec((None, S, D), lambda b, e, o: (b, 0, 0), pipeline_mode=once),
            scratch_shapes=[pltpu.VMEM((cap, D), F32), pltpu.VMEM((cap, 1), F32), pltpu.VMEM((cap, D), BF16)]),
        out_shape=jax.ShapeDtypeStruct((B, S, D), F32),
        compiler_params=pltpu.CompilerParams(dimension_semantics=("parallel", "arbitrary"),
                                             vmem_limit_bytes=MOE_VMEM_LIMIT),
        name="ec_moe",
    )(bounds, h2, slot, aff, w1, w3, w2)


def _final_kernel(x_ref, m_ref, g_ref, nw_ref, o_ref):
    m = m_ref[...]
    o_ref[...] = x_ref[...] + g_ref[...] * (m * lax.rsqrt(jnp.mean(m * m, axis=-1, keepdims=True) + NORM_EPS)
                                            * nw_ref[...])


def _final(x1, moe, g2, nw, tm):
    B, S, D = x1.shape
    tok = pl.BlockSpec((None, tm, D), lambda b, i: (b, i, 0))
    return pl.pallas_call(
        _final_kernel,
        grid=(B, S // tm),
        in_specs=[tok, tok, pl.BlockSpec((None, 1, D), lambda b, i: (b, 0, 0)), pl.BlockSpec((1, D), lambda b, i: (0, 0))],
        out_specs=tok,
        out_shape=jax.ShapeDtypeStruct((B, S, D), F32),
        compiler_params=pltpu.CompilerParams(dimension_semantics=("parallel", "parallel"),
                                             vmem_limit_bytes=VMEM_LIMIT),
        name="final_norm",
    )(x1, moe, g2, nw.reshape(1, D))


def _to_cm(t, rows):
    b, n, ch = t.shape
    return t.reshape(b, rows, GRID_W, ch).swapaxes(1, 2).reshape(b, n, ch)


def _from_cm(t, rows):
    b, n, ch = t.shape
    return t.reshape(b, GRID_W, rows, ch).swapaxes(1, 2).reshape(b, n, ch)


def kernel(x, c, ctx, c_ctx, ada_w, ada_b, norm1_pre, norm1_post, norm2_pre, norm2_post,
           w_in, rw_mu, rw_w0, rw_w2, rw_a0, rw_a2, rw_g2, rw_kk, rw_ka, rw_rk, rw_ln_w, rw_ln_b,
           ssm_conv_w, ssm_conv_b, ssm_dt_bias, ssm_a_log, ssm_d, ssm_norm, proj_a, proj_b, w_out,
           router, exp_w1, exp_w3, exp_w2):
    B, S, D = x.shape
    lc = ctx.shape[1]
    L = lc + S
    rows = S // GRID_W
    E = router.shape[-1]
    cap = 2 * S // E
    assert ada_w.shape[0] == 1, "single trunk layer"
    l = 0
    p = dict(rw_mu=rw_mu[l], rw_w0=rw_w0[l], rw_w2=rw_w2[l], rw_a0=rw_a0[l], rw_a2=rw_a2[l],
             rw_g2=rw_g2[l], rw_kk=rw_kk[l], rw_ka=rw_ka[l], rw_rk=rw_rk[l], rw_ln_w=rw_ln_w[l], rw_ln_b=rw_ln_b[l],
             ssm_conv_w=ssm_conv_w[l], ssm_conv_b=ssm_conv_b[l], ssm_dt_bias=ssm_dt_bias[l], ssm_a_log=ssm_a_log[l],
             ssm_d=ssm_d[l], ssm_norm=ssm_norm[l], proj_a=proj_a[l], proj_b=proj_b[l], w_out=w_out[l])
    big = lc % 256 == 0
    tt = 256 if big else 128
    chunks = 4 if big else 2
    n_rw = 3 * D + RW_LORA
    n_zx = 2 * D + (2 * D + 2 * SSM_GROUPS * SSM_STATE)
    n_dt = 2 * SSM_HEADS

    cond = jnp.concatenate([c, c_ctx[None], jnp.zeros((2 * HALO - B - 1, D), F32)], 0)
    mods_all = _adaln(cond, ada_w[l], ada_b[l], ada_w.shape[-1] // 4).reshape(2 * HALO, 6, D)
    mods, cmods = mods_all[:B], mods_all[B:B + 1]
    hl = _prenorm(x, mods, norm1_pre[l], tt)
    hc = _prenorm(ctx, cmods, norm1_pre[l], tt)
    seq_a = jnp.concatenate([hc, hl], 1).reshape(B * L, D)
    seq_b = jnp.concatenate([hc, _to_cm(hl, rows)], 1).reshape(B * L, D)
    w = w_in[l].astype(BF16)
    tm = 512 if (B * L) % 512 == 0 else 128
    proj_rw = _matmul(seq_a, w[:, :n_rw], tm, n_rw // 3, BF16).reshape(B, L, n_rw)
    gates = _matmul(hl.reshape(B * S, D), w[:, n_rw + n_zx + n_dt:], tm, D, BF16).reshape(B, S, 2 * D)
    proj_zx = _matmul(seq_b, w[:, n_rw:n_rw + n_zx], tm, D, BF16).reshape(B, L, n_zx)
    w_dt = jnp.pad(w[:, n_rw + n_zx:n_rw + n_zx + n_dt], ((0, 0), (0, LANES - n_dt)))
    proj_dt = _matmul(seq_b, w_dt, tm, LANES).reshape(B, L, LANES)[..., :n_dt]

    r, v, kk, lw, b2, ke, g, bv = _rwkv_prep(proj_rw, p, lc, tt)
    yf, ybw = _rwkv_scan(r, v, kk, lw, b2, ke, lc, chunks, 2)
    xbc = _ssm_conv(proj_zx, p['ssm_conv_w'], p['ssm_conv_b'], lc, tt, D, 2 * D)
    y2 = _ssd_scan(xbc, proj_dt, p, lc)
    yssm = _from_cm(_ssm_readout(y2, xbc, proj_zx, p, lc, tt), rows)

    x1, h2, aff = _merge(yf, ybw, bv, g, yssm, gates, x, mods, p, norm1_post[l], norm2_pre[l], router[l], lc, tt)
    slot, offs = _select(aff, cap)
    moe = _moe(h2, slot, offs, aff, exp_w1[l].astype(BF16), exp_w3[l].astype(BF16), exp_w2[l].astype(BF16), cap)
    return _final(x1, moe, mods[:, 5:6], norm2_post[l], tt)
```

```python
import functools
import math

import jax
import jax.numpy as jnp
from jax import lax
from jax.experimental import pallas as pl
from jax.experimental.pallas import tpu as pltpu

F32 = jnp.float32
BF16 = jnp.bfloat16

GRID_W = 64
NORM_EPS = 1e-6
RW_GN_EPS = 64e-5
HEAD = 64
LANES = 128
HALO = 16
RW_CHUNK = 64
RW_LORA = 3 * LANES
SSM_TAPS = 5
SSD_CHUNK = 128
SSM_HEADS, SSM_GROUPS, SSM_STATE = 32, 4, 128
VMEM_LIMIT = 48 * 1024 * 1024
MOE_VMEM_LIMIT = 56 * 1024 * 1024


def _dot(a, b):
    return jnp.dot(a.astype(BF16), b.astype(BF16), preferred_element_type=F32)


def _dot_nt(a, b):
    return lax.dot_general(a.astype(BF16), b.astype(BF16), (((1,), (1,)), ((), ())), preferred_element_type=F32)


def _split3(x):
    hi = x.astype(BF16)
    r1 = x - hi.astype(F32)
    mid = r1.astype(BF16)
    lo = (r1 - mid.astype(F32)).astype(BF16)
    return hi, mid, lo


def _dot_exact_lhs(m, x, terms=3):
    mb = m.astype(BF16)
    return sum(jnp.dot(mb, t, preferred_element_type=F32) for t in _split3(x)[:terms])


def _sigmoid(x):
    return 1.0 / (1.0 + jnp.exp(-x))


def _softplus(x):
    return jnp.maximum(x, 0.0) + jnp.log(1.0 + jnp.exp(-jnp.abs(x)))


def _head_sum(x, ones_bd):
    hi = x.astype(BF16)
    lo = (x - hi.astype(F32)).astype(BF16)
    ob = ones_bd.astype(BF16)
    n = x.shape[1] // LANES
    part = lambda t, i: jnp.dot(t[:, i * LANES:(i + 1) * LANES], ob, preferred_element_type=F32)
    return jnp.concatenate([part(hi, i) + part(lo, i) for i in range(n)], axis=1)


def _head_ones():
    i = jnp.arange(LANES)
    return ((i[:, None] // HEAD) == (i[None, :] // HEAD)).astype(F32)


def _halo_specs(tt, width, col, L):
    per = tt // HALO
    nh = L // HALO
    main = pl.BlockSpec((None, tt, width), lambda b, i, *_: (b, i, col(*_)))
    prev = pl.BlockSpec((None, HALO, width), lambda b, i, *_: (b, jnp.maximum(i * per - 1, 0), col(*_)))
    nxt = pl.BlockSpec((None, HALO, width), lambda b, i, *_: (b, jnp.minimum((i + 1) * per, nh - 1), col(*_)))
    return main, prev, nxt


def _seg_edges(i, tt, lc, L):
    t0 = i * tt
    has_prev = jnp.logical_and(t0 != 0, t0 != lc)
    has_next = jnp.logical_and(t0 + tt != lc, t0 + tt != L)
    return has_prev.astype(F32), has_next.astype(F32)


def _adaln_kernel(c_ref, w_ref, b_ref, o_ref):
    c = c_ref[...]
    o_ref[...] = jnp.dot(c * _sigmoid(c), w_ref[...], preferred_element_type=F32,
                         precision=lax.Precision.HIGHEST) + b_ref[...]


def _adaln(cond, w, b, tn):
    M, D = cond.shape
    N = w.shape[1]
    return pl.pallas_call(
        _adaln_kernel,
        grid=(N // tn,),
        in_specs=[pl.BlockSpec((M, D), lambda j: (0, 0)), pl.BlockSpec((D, tn), lambda j: (0, j)),
                  pl.BlockSpec((1, tn), lambda j: (0, j))],
        out_specs=pl.BlockSpec((M, tn), lambda j: (0, j)),
        out_shape=jax.ShapeDtypeStruct((M, N), F32),
        compiler_params=pltpu.CompilerParams(dimension_semantics=("parallel",), vmem_limit_bytes=VMEM_LIMIT),
        name="adaln",
    )(cond, w, b.reshape(1, N))


def _prenorm_kernel(x_ref, mod_ref, nw_ref, o_ref):
    x = x_ref[...]
    y = x * lax.rsqrt(jnp.mean(x * x, axis=-1, keepdims=True) + NORM_EPS) * nw_ref[...]
    o_ref[...] = (y * (1.0 + mod_ref[1:2, :]) + mod_ref[0:1, :]).astype(o_ref.dtype)


def _prenorm(x, mods, nw, tm):
    B, N, D = x.shape
    per_sample = mods.shape[0] == B
    tok = pl.BlockSpec((None, tm, D), lambda b, i: (b, i, 0))
    return pl.pallas_call(
        _prenorm_kernel,
        grid=(B, N // tm),
        in_specs=[tok, pl.BlockSpec((None, 6, D), (lambda b, i: (b, 0, 0)) if per_sample else (lambda b, i: (0, 0, 0))),
                  pl.BlockSpec((1, D), lambda b, i: (0, 0))],
        out_specs=tok,
        out_shape=jax.ShapeDtypeStruct((B, N, D), BF16),
        compiler_params=pltpu.CompilerParams(dimension_semantics=("parallel", "parallel"),
                                             vmem_limit_bytes=VMEM_LIMIT),
        name="prenorm",
    )(x, mods, nw.reshape(1, D))


def _matmul_kernel(a_ref, w_ref, o_ref):
    o_ref[...] = jnp.dot(a_ref[...], w_ref[...], preferred_element_type=F32).astype(o_ref.dtype)


def _matmul(a, w, tm, tn, out_dtype=F32):
    M, K = a.shape
    N = w.shape[1]
    assert M % tm == 0 and N % tn == 0
    return pl.pallas_call(
        _matmul_kernel,
        grid=(N // tn, M // tm),
        in_specs=[pl.BlockSpec((tm, K), lambda j, i: (i, 0)), pl.BlockSpec((K, tn), lambda j, i: (0, j))],
        out_specs=pl.BlockSpec((tm, tn), lambda j, i: (i, j)),
        out_shape=jax.ShapeDtypeStruct((M, N), out_dtype),
        compiler_params=pltpu.CompilerParams(dimension_semantics=("parallel", "parallel"),
                                             vmem_limit_bytes=VMEM_LIMIT),
        name="matmul",
    )(a, w)


def _rwkv_prep_kernel(p_ref, pp_ref, pn_ref, mu_ref, w2_ref, a2_ref, g2_ref, w0_ref, a0_ref, kkw_ref, ka_ref,
                      rk_ref, ones_ref, r_ref, v_ref, kk_ref, lw_ref, b_ref, ke_ref, g_ref, bv_ref, *, tt, lc, L):
    has_prev, has_next = _seg_edges(pl.program_id(1), tt, lc, L)
    rows = lax.broadcasted_iota(jnp.int32, (tt, 1), 0)
    first, last = rows == 0, rows == tt - 1

    def shifted(lo, hi):
        u = p_ref[:, lo:hi].astype(F32)
        pv = pp_ref[HALO - 1:HALO, lo:hi].astype(F32) * has_prev
        nx = pn_ref[0:1, lo:hi].astype(F32) * has_next
        up = jnp.where(first, pv, pltpu.roll(u, 1, 0))
        un = jnp.where(last, nx, pltpu.roll(u, tt - 1, 0))
        return u + mu_ref[0:1, lo:hi] * (up - u) + mu_ref[1:2, lo:hi] * (un - u)

    C = r_ref.shape[-1]
    lane = lax.broadcasted_iota(jnp.int32, (1, LANES), 1)
    half = [(lane < HEAD).astype(F32), (lane >= HEAD).astype(F32)]
    xw = jnp.tanh(shifted(3 * C, 3 * C + LANES))
    xa = shifted(3 * C + LANES, 3 * C + 2 * LANES)
    xg = _sigmoid(shifted(3 * C + 2 * LANES, 3 * C + 3 * LANES))
    g_ref[...] = _dot(xg, g2_ref[...]).astype(g_ref.dtype)
    r = shifted(0, C)
    k = shifted(C, 2 * C)
    v = shifted(2 * C, 3 * C)
    r_ref[...] = r.astype(r_ref.dtype)
    v_ref[...] = v.astype(v_ref.dtype)
    ones_bd = ones_ref[...]
    kkr = k * kkw_ref[...]
    kk = kkr * lax.rsqrt(jnp.maximum(_head_sum(kkr * kkr, ones_bd), 1e-12))
    kk_ref[...] = kk.astype(kk_ref.dtype)
    ksum = jnp.zeros_like(k)
    for d in (0, 1):
        w_raw = w0_ref[d:d + 1, :] + _dot(xw * half[d], w2_ref[...])
        lw_ref[d] = -math.exp(-0.5) * _sigmoid(w_raw)
        a = _sigmoid(a0_ref[d:d + 1, :] + _dot(xa * half[d], a2_ref[...]))
        b_ref[d] = (kk * a).astype(b_ref.dtype)
        ke = k * (1.0 + (a - 1.0) * ka_ref[...])
        ke_ref[d] = ke.astype(ke_ref.dtype)
        ksum = ksum + ke
    bv_ref[...] = (_head_sum(r * ksum * rk_ref[...], ones_bd) * v).astype(bv_ref.dtype)


def _rwkv_prep(proj, p, lc, tt):
    B, L, W = proj.shape
    C = (W - RW_LORA) // 3
    assert L % tt == 0 and lc % tt == 0
    main, prev, nxt = _halo_specs(tt, W, lambda: 0, L)
    full = lambda a: pl.BlockSpec(a.shape, lambda b, i: (0,) * a.ndim)
    w2 = p['rw_w2'].reshape(2 * HEAD, C)
    a2 = p['rw_a2'].reshape(2 * HEAD, C)
    params = [p['rw_mu'], w2, a2, p['rw_g2'], p['rw_w0'], p['rw_a0'], p['rw_kk'].reshape(1, C),
              p['rw_ka'].reshape(1, C), p['rw_rk'].reshape(1, C), _head_ones()]
    one = pl.BlockSpec((None, tt, C), lambda b, i: (b, i, 0))
    two = pl.BlockSpec((2, None, tt, C), lambda b, i: (0, b, i, 0))
    s1 = jax.ShapeDtypeStruct((B, L, C), BF16)
    s2 = jax.ShapeDtypeStruct((2, B, L, C), BF16)
    return pl.pallas_call(
        functools.partial(_rwkv_prep_kernel, tt=tt, lc=lc, L=L),
        grid=(B, L // tt),
        in_specs=[main, prev, nxt] + [full(a) for a in params],
        out_specs=[one, one, one, two, two, two, one, one],
        out_shape=[s1, s1, s1, jax.ShapeDtypeStruct((2, B, L, C), F32), s2, s2, s1, s1],
        compiler_params=pltpu.CompilerParams(dimension_semantics=("parallel", "parallel"),
                                             vmem_limit_bytes=VMEM_LIMIT),
        name="rwkv_prep",
    )(proj, proj, proj, *params)


def _rwkv_masks():
    T = RW_CHUNK
    out = []
    for rev in (False, True):
        t = jnp.arange(T)
        t = (T - 1 - t) if rev else t
        rt, ct = t[:, None], jnp.tile(t, 2)[None, :]
        ms = [rt > ct, rt >= ct]
        s = 1
        while s < T:
            br, bc = rt // s, ct // s
            ms.append((br // 2 == bc // 2) & (br % 2 == 1) & (bc % 2 == 0))
            s *= 2
        out.append(jnp.stack(ms))
    return jnp.stack(out).astype(F32)


def _rwkv_scan_kernel(m_ref, rf_ref, vf_ref, kkf_ref, rb_ref, vb_ref, kkb_ref, lwf_ref, bf_ref, kf_ref,
                      lwb_ref, bb_ref, kb_ref, yf_ref, yb_ref, st_ref, *, chunks, pairs):
    T = RW_CHUNK

    @pl.when(pl.program_id(2) == 0)
    def _():
        st_ref[...] = jnp.zeros_like(st_ref)

    row = lax.broadcasted_iota(jnp.int32, (2 * T, 2 * T), 0)
    col = lax.broadcasted_iota(jnp.int32, (2 * T, 2 * T), 1)
    bd = ((row >> 6) == (col >> 6)).astype(F32)
    trow = lax.broadcasted_iota(jnp.int32, (T, LANES), 0)
    tcol = lax.broadcasted_iota(jnp.int32, (T, LANES), 1)
    eye = (trow == (tcol & (T - 1))).astype(F32)
    h0f = (tcol < HEAD).astype(F32)
    h1f = 1.0 - h0f
    bdize = lambda z: jnp.concatenate([z * h0f, z * h1f], axis=0)
    refs = ((rf_ref, vf_ref, kkf_ref, lwf_ref, bf_ref, kf_ref, yf_ref),
            (rb_ref, vb_ref, kkb_ref, lwb_ref, bb_ref, kb_ref, yb_ref))
    chains = [(d, q) for d in (0, 1) for q in range(pairs)]
    order = {0: list(range(chunks)), 1: list(range(chunks - 1, -1, -1))}
    units = [(d, q, j) for j in range(chunks) for (d, q) in chains]
    n_levels = RW_CHUNK.bit_length() - 1

    def tile(ref, d, q, j):
        cj = order[d][j]
        return ref[cj * T:(cj + 1) * T, q * LANES:(q + 1) * LANES].astype(F32)

    A = {}
    for un in units:
        d, q, j = un
        r_ref, v_ref, kk_ref, lw_ref, b_ref, k_ref, _ = refs[d]
        lw = tile(lw_ref, d, q, j)
        c = _dot_exact_lhs(m_ref[d, 1, :, :T], lw, terms=2)
        gl = jnp.exp(jnp.sum(lw, axis=0, keepdims=True))
        e_nc = jnp.exp(-c)
        at = -tile(kk_ref, d, q, j) * jnp.exp(c - lw)
        rtl = tile(r_ref, d, q, j) * jnp.exp(c)
        bt = tile(b_ref, d, q, j) * e_nc
        kt = tile(k_ref, d, q, j) * e_nc
        A[un] = dict(gl=gl, ar=jnp.concatenate([at, rtl], axis=0).astype(BF16),
                     bk=jnp.concatenate([bt * gl, kt * gl], axis=0).astype(BF16),
                     rhs=jnp.concatenate([bdize(bt), bdize(kt)], axis=0).astype(BF16))
    for un in units:
        a = A[un]
        a['aa'] = lax.dot_general(a['ar'], a.pop('rhs'), (((1,), (1,)), ((), ())),
                                  preferred_element_type=F32)
    for un in units:
        d = un[0]
        a = A[un]
        aa = a.pop('aa')
        a['n'] = aa[:T, :2 * T] * m_ref[d, 0]
        a['ak'] = (aa[:T, 2 * T:] * m_ref[d, 0]).astype(BF16)
        a['rbrk'] = jnp.concatenate([aa[T:, :2 * T] * m_ref[d, 1], aa[T:, 2 * T:] * m_ref[d, 1]], axis=1).astype(BF16)
        a['p'] = eye + a['n'] * m_ref[d, 2]
    for lv in range(1, n_levels):
        for un in units:
            a = A[un]
            a['pn'] = _dot(a['p'], bdize(a['n'] * m_ref[un[0], 2 + lv]))
        for un in units:
            a = A[un]
            a['p'] = a['p'] + _dot(a.pop('pn'), bdize(a['p']))
    for un in units:
        d, q, j = un
        a = A[un]
        a['vbd'] = bdize(tile(refs[d][1], d, q, j)).astype(BF16)
        a['akv'] = jnp.dot(a.pop('ak'), a['vbd'], preferred_element_type=F32)
        a['p'] = a['p'].astype(BF16)
        a.pop('n')

    st = {ch: st_ref[ci] for ci, ch in enumerate(chains)}
    for j in range(chunks):
        x = {}
        for ch in chains:
            x[ch] = _dot_nt(A[ch + (j,)]['ar'], st[ch])
        u = {}
        for ch in chains:
            a = A[ch + (j,)]
            u[ch] = _dot(a['p'], bdize(x[ch][:T] + a['akv']))
        for ch in chains:
            d, q = ch
            a = A[ch + (j,)]
            y = x[ch][T:] + jnp.dot(a['rbrk'], jnp.concatenate([bdize(u[ch]).astype(BF16), a['vbd']], axis=0),
                                    preferred_element_type=F32)
            cj = order[d][j]
            refs[d][6][cj * T:(cj + 1) * T, q * LANES:(q + 1) * LANES] = y.astype(yf_ref.dtype)
            uv = jnp.concatenate([u[ch], tile(refs[d][1], d, q, j)], axis=0)
            st[ch] = (st[ch] * a['gl'] + _dot(uv.T, a['bk'])) * bd
    for ci, ch in enumerate(chains):
        st_ref[ci] = st[ch]


def _rwkv_scan(r, v, kk, lw, b2, k2, lc, chunks, pairs):
    B, L, C = r.shape
    tb = chunks * RW_CHUNK
    wl = pairs * LANES
    assert L % tb == 0 and lc % tb == 0 and C % wl == 0
    nblk, nctx = L // tb, lc // tb

    def bwd(i):
        return jnp.where(i < nctx, nctx - 1 - i, nblk - 1 - (i - nctx))

    sf = pl.BlockSpec((None, tb, wl), lambda b, p, i: (b, i, p))
    sb = pl.BlockSpec((None, tb, wl), lambda b, p, i: (b, bwd(i), p))
    df = pl.BlockSpec((None, None, tb, wl), lambda b, p, i: (0, b, i, p))
    db = pl.BlockSpec((None, None, tb, wl), lambda b, p, i: (1, b, bwd(i), p))
    masks = _rwkv_masks()
    return pl.pallas_call(
        functools.partial(_rwkv_scan_kernel, chunks=chunks, pairs=pairs),
        grid=(B, C // wl, nblk),
        in_specs=[pl.BlockSpec(masks.shape, lambda b, p, i: (0, 0, 0, 0)),
                  sf, sf, sf, sb, sb, sb, df, df, df, db, db, db],
        out_specs=[sf, sb],
        out_shape=[jax.ShapeDtypeStruct((B, L, C), BF16)] * 2,
        scratch_shapes=[pltpu.VMEM((2 * pairs, LANES, LANES), F32)],
        compiler_params=pltpu.CompilerParams(
            dimension_semantics=("parallel", "parallel", "arbitrary"),
            vmem_limit_bytes=VMEM_LIMIT),
        name="rwkv_scan",
    )(masks, r, v, kk, r, v, kk, lw, b2, k2, lw, b2, k2)


def _ssm_conv_kernel(x_ref, xp_ref, xn_ref, w_ref, b_ref, o_ref, *, tt, lc, L):
    has_prev, has_next = _seg_edges(pl.program_id(1), tt, lc, L)
    ext = jnp.concatenate([xp_ref[...].astype(F32) * has_prev, x_ref[...].astype(F32),
                           xn_ref[...].astype(F32) * has_next], axis=0)
    n = tt + 2 * HALO
    acc = jnp.zeros(x_ref.shape, F32) + b_ref[...]
    for j in range(SSM_TAPS):
        off = j - SSM_TAPS // 2
        sh = ext if off == 0 else pltpu.roll(ext, (-off) % n, 0)
        acc = acc + sh[HALO:HALO + tt] * w_ref[j:j + 1, :]
    o_ref[...] = (acc * _sigmoid(acc)).astype(o_ref.dtype)


def _ssm_conv(zx, w, b, lc, tt, tw, col0):
    B, L, _ = zx.shape
    cc = w.shape[0]
    assert cc % tw == 0 and col0 % tw == 0 and L % tt == 0 and lc % tt == 0
    main, prev, nxt = _halo_specs(tt, tw, lambda c: c + col0 // tw, L)
    wt = jnp.transpose(w)
    return pl.pallas_call(
        functools.partial(_ssm_conv_kernel, tt=tt, lc=lc, L=L),
        grid=(B, L // tt, cc // tw),
        in_specs=[main, prev, nxt, pl.BlockSpec((SSM_TAPS, tw), lambda b, i, c: (0, c)),
                  pl.BlockSpec((1, tw), lambda b, i, c: (0, c))],
        out_specs=pl.BlockSpec((None, tt, tw), lambda b, i, c: (b, i, c)),
        out_shape=jax.ShapeDtypeStruct((B, L, cc), BF16),
        compiler_params=pltpu.CompilerParams(dimension_semantics=("parallel", "parallel", "parallel"),
                                             vmem_limit_bytes=VMEM_LIMIT),
        name="ssm_conv",
    )(zx, zx, zx, wt, b.reshape(1, cc))


def _ssd_scan_kernel(x_ref, bm_ref, cm_ref, dt_ref, dtt_ref, a_ref, at_ref, bias_ref, biast_ref, y_ref,
                     ht_ref, *, groups, hpg):
    Q = SSD_CHUNK
    rev = pl.program_id(1) == 1

    @pl.when(pl.program_id(2) == 0)
    def _():
        ht_ref[...] = jnp.zeros_like(ht_ref)

    row = lax.broadcasted_iota(jnp.int32, (Q, Q), 0)
    col = lax.broadcasted_iota(jnp.int32, (Q, Q), 1)
    incl = jnp.where(rev, col - row, row - col) >= 0
    tri = incl.astype(F32)
    dt = _softplus(dt_ref[...] + bias_ref[...])
    cum = _dot_exact_lhs(tri, dt * a_ref[...])
    dtt = _softplus(dtt_ref[...] + biast_ref[...])
    hi, mid, lo = _split3(dtt * at_ref[...])
    trib = tri.astype(BF16)
    nt = lambda a_, b_: lax.dot_general(a_, b_, (((1,), (1,)), ((), ())), preferred_element_type=F32)
    cumt = nt(hi, trib) + nt(mid, trib) + nt(lo, trib)
    head0 = lax.broadcasted_iota(jnp.int32, (1, LANES), 1) < HEAD
    h0f = head0.astype(F32)
    h1f = 1.0 - h0f
    for g in range(groups):
        bm = bm_ref[:, g * SSM_STATE:(g + 1) * SSM_STATE]
        cm = cm_ref[:, g * SSM_STATE:(g + 1) * SSM_STATE]
        cb = _dot_nt(cm, bm)
        bmt = bm.astype(F32).T
        for pp in range(hpg // 2):
            e0 = g * hpg + 2 * pp
            lanes = slice((e0 // 2) * LANES, (e0 // 2 + 1) * LANES)
            cum_l = jnp.where(head0, cum[:, e0:e0 + 1], cum[:, e0 + 1:e0 + 2])
            dt_l = jnp.where(head0, dt[:, e0:e0 + 1], dt[:, e0 + 1:e0 + 2])
            ms = []
            for e in (e0, e0 + 1):
                seg = jnp.exp(jnp.where(incl, cum[:, e:e + 1] - cumt[e:e + 1, :], -jnp.inf))
                ms.append(cb * seg)
            xdt = x_ref[:, lanes].astype(F32) * dt_l
            clast = jnp.min(cum_l, axis=0, keepdims=True)
            htp = ht_ref[:, lanes]
            y = _dot(jnp.concatenate(ms, axis=1), jnp.concatenate([xdt * h0f, xdt * h1f], axis=0))
            y_ref[:, lanes] = (y + _dot(cm, htp) * jnp.exp(cum_l)).astype(y_ref.dtype)
            ht_ref[:, lanes] = htp * jnp.exp(clast) + _dot(bmt, xdt * jnp.exp(clast - cum_l))


def _ssd_scan(xbc, dtp, p, lc):
    B, L, _ = xbc.shape
    Q = SSD_CHUNK
    H, G, N = SSM_HEADS, SSM_GROUPS, SSM_STATE
    xw = H * HEAD
    gw = G * N
    assert L % Q == 0 and lc % Q == 0 and xw % gw == 0
    nck, nctx = L // Q, lc // Q
    dt = jnp.pad(jnp.moveaxis(dtp.reshape(B, L, 2, H), 2, 1), ((0, 0), (0, 0), (0, 0), (0, LANES - H)))
    dtt = jnp.swapaxes(dt, 2, 3)
    a = jnp.pad(-jnp.exp(p['ssm_a_log']), ((0, 0), (0, LANES - H)))
    bias = jnp.pad(p['ssm_dt_bias'], ((0, 0), (0, LANES - H)))

    def ck(d, i):
        bwd = jnp.where(i < nctx, nctx - 1 - i, nck - 1 - (i - nctx))
        return jnp.where(d == 0, i, bwd)

    return pl.pallas_call(
        functools.partial(_ssd_scan_kernel, groups=G, hpg=H // G),
        grid=(B, 2, nck),
        in_specs=[pl.BlockSpec((None, Q, xw), lambda b, d, i: (b, ck(d, i), 0)),
                  pl.BlockSpec((None, Q, gw), lambda b, d, i: (b, ck(d, i), xw // gw)),
                  pl.BlockSpec((None, Q, gw), lambda b, d, i: (b, ck(d, i), xw // gw + 1)),
                  pl.BlockSpec((None, None, Q, LANES), lambda b, d, i: (b, d, ck(d, i), 0)),
                  pl.BlockSpec((None, None, LANES, Q), lambda b, d, i: (b, d, 0, ck(d, i))),
                  pl.BlockSpec((None, 1, LANES), lambda b, d, i: (d, 0, 0)),
                  pl.BlockSpec((None, LANES, 1), lambda b, d, i: (d, 0, 0)),
                  pl.BlockSpec((None, 1, LANES), lambda b, d, i: (d, 0, 0)),
                  pl.BlockSpec((None, LANES, 1), lambda b, d, i: (d, 0, 0))],
        out_specs=pl.BlockSpec((None, None, Q, xw), lambda b, d, i: (d, b, ck(d, i), 0)),
        out_shape=jax.ShapeDtypeStruct((2, B, L, xw), BF16),
        scratch_shapes=[pltpu.VMEM((N, xw), F32)],
        compiler_params=pltpu.CompilerParams(
            dimension_semantics=("parallel", "arbitrary", "arbitrary"),
            vmem_limit_bytes=VMEM_LIMIT),
        name="ssd_scan",
    )(xbc, xbc, xbc, dt, dtt, a.reshape(2, 1, LANES), a.reshape(2, LANES, 1), bias.reshape(2, 1, LANES),
      bias.reshape(2, LANES, 1))


def _ssm_readout_kernel(yf_ref, yb_ref, x_ref, z_ref, d_ref, nw_ref, o_ref):
    z = z_ref[...].astype(F32)
    y = (yf_ref[...].astype(F32) + yb_ref[...].astype(F32) + d_ref[...] * x_ref[...].astype(F32)) * (z * _sigmoid(z))
    o_ref[...] = (y * lax.rsqrt(jnp.mean(y * y, axis=-1, keepdims=True) + NORM_EPS) * nw_ref[...]).astype(o_ref.dtype)


def _ssm_readout(y2, xbc, zx, p, lc, tt):
    _, B, L, C = y2.shape
    S = L - lc
    off = lc // tt
    d_l = jnp.repeat(p['ssm_d'], HEAD).reshape(1, C)
    row = lambda b, i: (b, i + off, 0)
    return pl.pallas_call(
        _ssm_readout_kernel,
        grid=(B, S // tt),
        in_specs=[pl.BlockSpec((None, None, tt, C), lambda b, i: (0, b, i + off, 0)),
                  pl.BlockSpec((None, None, tt, C), lambda b, i: (1, b, i + off, 0)),
                  pl.BlockSpec((None, tt, C), row), pl.BlockSpec((None, tt, C), row),
                  pl.BlockSpec((1, C), lambda b, i: (0, 0)), pl.BlockSpec((1, C), lambda b, i: (0, 0))],
        out_specs=pl.BlockSpec((None, tt, C), lambda b, i: (b, i, 0)),
        out_shape=jax.ShapeDtypeStruct((B, S, C), BF16),
        compiler_params=pltpu.CompilerParams(dimension_semantics=("parallel", "parallel"),
                                             vmem_limit_bytes=VMEM_LIMIT),
        name="ssm_readout",
    )(y2, y2, xbc, zx, d_l, p['ssm_norm'].reshape(1, C))


def _merge_kernel(yf_ref, yb_ref, bv_ref, g_ref, ys_ref, gt_ref, x_ref, mod_ref, lnw_ref, lnb_ref, n1_ref, n2_ref,
                  pa_ref, pb_ref, wo_ref, rt_ref, ones_ref, x1_ref, h2_ref, aff_ref):
    C = yf_ref.shape[-1]
    ones_bd = ones_ref[...]
    y = yf_ref[...].astype(F32) + yb_ref[...].astype(F32)
    mean = _head_sum(y, ones_bd) * (1.0 / HEAD)
    yc = y - mean
    var = _head_sum(yc * yc, ones_bd) * (1.0 / HEAD)
    ya = (yc * lax.rsqrt(var + RW_GN_EPS) * lnw_ref[...] + lnb_ref[...] + bv_ref[...].astype(F32)) * g_ref[...].astype(F32)
    gt = gt_ref[...].astype(F32)
    mix = _sigmoid(gt[:, :C]) * _dot(ya, pa_ref[...]) + _sigmoid(gt[:, C:]) * jnp.dot(
        ys_ref[...], pb_ref[...], preferred_element_type=F32)
    ml = _dot(mix, wo_ref[...])
    rms = lambda t: t * lax.rsqrt(jnp.mean(t * t, axis=-1, keepdims=True) + NORM_EPS)
    x1 = x_ref[...] + mod_ref[2:3, :] * (rms(ml) * n1_ref[...])
    x1_ref[...] = x1
    h2 = (rms(x1) * n2_ref[...] * (1.0 + mod_ref[4:5, :]) + mod_ref[3:4, :]).astype(BF16)
    h2_ref[...] = h2
    logits = lax.dot_general(rt_ref[...], h2, (((1,), (1,)), ((), ())), preferred_element_type=F32)
    e = jnp.exp(logits - jnp.max(logits, axis=0, keepdims=True))
    aff_ref[...] = e / jnp.sum(e, axis=0, keepdims=True)


def _merge(yf, yb, bv, g, yssm, gates, x, mods, p, n1post, n2pre, router, lc, tm):
    B, S, D = x.shape
    C = yf.shape[-1]
    E = router.shape[1]
    off = lc // tm
    lat = pl.BlockSpec((None, tm, C), lambda b, i: (b, i + off, 0))
    tok = lambda w: pl.BlockSpec((None, tm, w), lambda b, i: (b, i, 0))
    full = lambda a: pl.BlockSpec(a.shape, lambda b, i: (0,) * a.ndim, pipeline_mode=pl.Buffered(1))
    row = lambda a: a.reshape(1, -1)
    params = [row(p['rw_ln_w']), row(p['rw_ln_b']), row(n1post), row(n2pre), p['proj_a'].astype(BF16),
              p['proj_b'].astype(BF16), p['w_out'].astype(BF16), jnp.transpose(router).astype(BF16), _head_ones()]
    return pl.pallas_call(
        _merge_kernel,
        grid=(B, S // tm),
        in_specs=[lat, lat, lat, lat, tok(yssm.shape[-1]), tok(2 * C), tok(D),
                  pl.BlockSpec((None, 6, D), lambda b, i: (b, 0, 0))] + [full(a) for a in params],
        out_specs=[tok(D), tok(D), pl.BlockSpec((None, E, tm), lambda b, i: (b, 0, i))],
        out_shape=[jax.ShapeDtypeStruct((B, S, D), F32), jax.ShapeDtypeStruct((B, S, D), BF16),
                   jax.ShapeDtypeStruct((B, E, S), F32)],
        compiler_params=pltpu.CompilerParams(dimension_semantics=("parallel", "parallel"),
                                             vmem_limit_bytes=VMEM_LIMIT),
        name="merge",
    )(yf, yb, bv, g, yssm, gates, x, mods, *params)


def _select_kernel(aff_ref, slot_ref, off_ref, *, cap):
    aff = aff_ref[...]
    E, S = aff.shape
    bits = lax.bitcast_convert_type(aff, jnp.int32)
    count = lambda m: jnp.sum(m.astype(F32), axis=1, keepdims=True)
    thr = jnp.zeros((E, 1), jnp.int32)
    for bit in range(30, -1, -1):
        cand = thr | (1 << bit)
        thr = jnp.where(count(bits >= cand) >= cap, cand, thr)
    gt = bits > thr
    eq = bits == thr
    need = cap - count(gt)

    tl = lax.broadcasted_iota(jnp.int32, (LANES, LANES), 0)
    tc = lax.broadcasted_iota(jnp.int32, (LANES, LANES), 1)
    before = (tl < tc).astype(BF16)

    def prefix(m):
        outs, carry = [], jnp.zeros((E, 1), F32)
        for t in range(S // LANES):
            mt = m[:, t * LANES:(t + 1) * LANES]
            outs.append(jnp.dot(mt.astype(BF16), before, preferred_element_type=F32) + carry)
            carry = carry + jnp.sum(mt, axis=1, keepdims=True)
        return jnp.concatenate(outs, axis=1)

    eqf = eq.astype(F32)
    taken = gt.astype(F32) + eqf * (prefix(eqf) < need).astype(F32)
    slot_ref[...] = jnp.where(taken > 0.5, prefix(taken), -1.0).astype(jnp.int32)
    tok = lax.broadcasted_iota(jnp.int32, (S, LANES), 0)
    tile_start = lax.broadcasted_iota(jnp.int32, (S, LANES), 1) * LANES
    off_ref[...] = jnp.dot(taken.astype(BF16), (tok < tile_start).astype(BF16),
                           preferred_element_type=F32).astype(jnp.int32)


def _select(aff, cap):
    B, E, S = aff.shape
    spec = pl.BlockSpec((None, E, S), lambda b: (b, 0, 0))
    return pl.pallas_call(
        functools.partial(_select_kernel, cap=cap),
        grid=(B,), in_specs=[spec], out_specs=[spec, pl.BlockSpec((None, E, LANES), lambda b: (b, 0, 0))],
        out_shape=[jax.ShapeDtypeStruct((B, E, S), jnp.int32), jax.ShapeDtypeStruct((B, E, LANES), jnp.int32)],
        compiler_params=pltpu.CompilerParams(dimension_semantics=("parallel",), vmem_limit_bytes=VMEM_LIMIT),
        name="ec_select",
    )(aff)


def _moe_kernel(off_ref, h_ref, slot_ref, aff_ref, w1_ref, w3_ref, w2_ref, o_ref, xe_ref, gate_ref, ye_ref,
                *, cap, ts, sb):
    b, e = pl.program_id(0), pl.program_id(1)
    S, D = h_ref.shape
    nt, nsb = S // ts, cap // sb
    base = (b * pl.num_programs(1) + e) * (nt + 1)
    slot_iota = lax.broadcasted_iota(jnp.int32, (sb, ts), 0)

    def pairs(body):
        for t in range(nt):
            lo, hi = off_ref[base + t], off_ref[base + t + 1]
            for k in range(nsb):
                def run(t=t, k=k):
                    tok = slice(t * ts, (t + 1) * ts)
                    hit = (slot_iota + k * sb) == slot_ref[pl.ds(e, 1), tok]
                    body(hit, tok, slice(k * sb, (k + 1) * sb))
                pl.when(jnp.logical_and(lo < (k + 1) * sb, hi > k * sb))(run)

    xe_ref[...] = jnp.zeros_like(xe_ref)
    gate_ref[...] = jnp.zeros_like(gate_ref)

    @pl.when(e == 0)
    def _():
        o_ref[...] = jnp.zeros_like(o_ref)

    def gather(hit, tok, rows):
        gate_ref[rows, :] += jnp.sum(jnp.where(hit, aff_ref[pl.ds(e, 1), tok], 0.0), axis=1, keepdims=True)
        xe_ref[rows, :] += jnp.dot(jnp.where(hit, 1.0, 0.0).astype(BF16), h_ref[tok, :],
                                   preferred_element_type=F32)

    pairs(gather)
    xe = xe_ref[...].astype(BF16)
    a1 = jnp.dot(xe, w1_ref[...], preferred_element_type=F32)
    a3 = jnp.dot(xe, w3_ref[...], preferred_element_type=F32)
    hid = (a1 * _sigmoid(a1)) * a3
    ye_ref[...] = (_dot(hid, w2_ref[...]) * gate_ref[...]).astype(BF16)

    def scatter(hit, tok, rows):
        o_ref[tok, :] += lax.dot_general(jnp.where(hit, 1.0, 0.0).astype(BF16), ye_ref[rows, :],
                                         (((0,), (0,)), ((), ())), preferred_element_type=F32)

    pairs(scatter)


def _moe(h2, slot, offs, aff, w1, w3, w2, cap):
    B, S, D = h2.shape
    E = slot.shape[1]
    F = w1.shape[-1]
    ts, sb = min(S, 512), min(cap, LANES)
    nt = S // ts
    bounds = jnp.concatenate([offs[..., ::ts // LANES][..., :nt], jnp.full((B, E, 1), cap, jnp.int32)], -1).reshape(-1)
    once = pl.Buffered(1)
    return pl.pallas_call(
        functools.partial(_moe_kernel, cap=cap, ts=ts, sb=sb),
        grid_spec=pltpu.PrefetchScalarGridSpec(
            num_scalar_prefetch=1,
            grid=(B, E),
            in_specs=[pl.BlockSpec((None, S, D), lambda b, e, o: (b, 0, 0), pipeline_mode=once),
                      pl.BlockSpec((None, E, S), lambda b, e, o: (b, 0, 0), pipeline_mode=once),
                      pl.BlockSpec((None, E, S), lambda b, e, o: (b, 0, 0), pipeline_mode=once),
                      pl.BlockSpec((None, D, F), lambda b, e, o: (e, 0, 0)),
                      pl.BlockSpec((None, D, F), lambda b, e, o: (e, 0, 0)),
                      pl.BlockSpec((None, F, D), lambda b, e, o: (e, 0, 0))],
            out_specs=pl.BlockSpec((None, S, D), lambda b, e, o: (b, 0, 0), pipeline_mode=once),
            scratch_shapes=[pltpu.VMEM((cap, D), F32), pltpu.VMEM((cap, 1), F32), pltpu.VMEM((cap, D), BF16)]),
        out_shape=jax.ShapeDtypeStruct((B, S, D), F32),
        compiler_params=pltpu.CompilerParams(dimension_semantics=("parallel", "arbitrary"),
                                             vmem_limit_bytes=MOE_VMEM_LIMIT),
        name="ec_moe",
    )(bounds, h2, slot, aff, w1, w3, w2)


def _final_kernel(x_ref, m_ref, g_ref, nw_ref, o_ref):
    m = m_ref[...]
    o_ref[...] = x_ref[...] + g_ref[...] * (m * lax.rsqrt(jnp.mean(m * m, axis=-1, keepdims=True) + NORM_EPS)
                                            * nw_ref[...])


def _final(x1, moe, g2, nw, tm):
    B, S, D = x1.shape
    tok = pl.BlockSpec((None, tm, D), lambda b, i: (b, i, 0))
    return pl.pallas_call(
        _final_kernel,
        grid=(B, S // tm),
        in_specs=[tok, tok, pl.BlockSpec((None, 1, D), lambda b, i: (b, 0, 0)), pl.BlockSpec((1, D), lambda b, i: (0, 0))],
        out_specs=tok,
        out_shape=jax.ShapeDtypeStruct((B, S, D), F32),
        compiler_params=pltpu.CompilerParams(dimension_semantics=("parallel", "parallel"),
                                             vmem_limit_bytes=VMEM_LIMIT),
        name="final_norm",
    )(x1, moe, g2, nw.reshape(1, D))


def _to_cm(t, rows):
    b, n, ch = t.shape
    return t.reshape(b, rows, GRID_W, ch).swapaxes(1, 2).reshape(b, n, ch)


def _from_cm(t, rows):
    b, n, ch = t.shape
    return t.reshape(b, GRID_W, rows, ch).swapaxes(1, 2).reshape(b, n, ch)


def kernel(x, c, ctx, c_ctx, ada_w, ada_b, norm1_pre, norm1_post, norm2_pre, norm2_post,
           w_in, rw_mu, rw_w0, rw_w2, rw_a0, rw_a2, rw_g2, rw_kk, rw_ka, rw_rk, rw_ln_w, rw_ln_b,
           ssm_conv_w, ssm_conv_b, ssm_dt_bias, ssm_a_log, ssm_d, ssm_norm, proj_a, proj_b, w_out,
           router, exp_w1, exp_w3, exp_w2):
    B, S, D = x.shape
    lc = ctx.shape[1]
    L = lc + S
    rows = S // GRID_W
    E = router.shape[-1]
    cap = 2 * S // E
    assert ada_w.shape[0] == 1, "single trunk layer"
    l = 0
    p = dict(rw_mu=rw_mu[l], rw_w0=rw_w0[l], rw_w2=rw_w2[l], rw_a0=rw_a0[l], rw_a2=rw_a2[l],
             rw_g2=rw_g2[l], rw_kk=rw_kk[l], rw_ka=rw_ka[l], rw_rk=rw_rk[l], rw_ln_w=rw_ln_w[l], rw_ln_b=rw_ln_b[l],
             ssm_conv_w=ssm_conv_w[l], ssm_conv_b=ssm_conv_b[l], ssm_dt_bias=ssm_dt_bias[l], ssm_a_log=ssm_a_log[l],
             ssm_d=ssm_d[l], ssm_norm=ssm_norm[l], proj_a=proj_a[l], proj_b=proj_b[l], w_out=w_out[l])
    big = lc % 256 == 0
    tt = 256 if big else 128
    chunks = 4 if big else 2
    n_rw = 3 * D + RW_LORA
    n_zx = 2 * D + (2 * D + 2 * SSM_GROUPS * SSM_STATE)
    n_dt = 2 * SSM_HEADS

    cond = jnp.concatenate([c, c_ctx[None], jnp.zeros((2 * HALO - B - 1, D), F32)], 0)
    mods_all = _adaln(cond, ada_w[l], ada_b[l], ada_w.shape[-1] // 4).reshape(2 * HALO, 6, D)
    mods, cmods = mods_all[:B], mods_all[B:B + 1]
    hl = _prenorm(x, mods, norm1_pre[l], tt)
    hc = _prenorm(ctx, cmods, norm1_pre[l], tt)
    seq_a = jnp.concatenate([hc, hl], 1).reshape(B * L, D)
    seq_b = jnp.concatenate([hc, _to_cm(hl, rows)], 1).reshape(B * L, D)
    w = w_in[l].astype(BF16)
    tm = 512 if (B * L) % 512 == 0 else 128
    proj_rw = _matmul(seq_a, w[:, :n_rw], tm, n_rw // 3, BF16).reshape(B, L, n_rw)
    gates = _matmul(hl.reshape(B * S, D), w[:, n_rw + n_zx + n_dt:], tm, D, BF16).reshape(B, S, 2 * D)
    proj_zx = _matmul(seq_b, w[:, n_rw:n_rw + n_zx], tm, D, BF16).reshape(B, L, n_zx)
    w_dt = jnp.pad(w[:, n_rw + n_zx:n_rw + n_zx + n_dt], ((0, 0), (0, LANES - n_dt)))
    proj_dt = _matmul(seq_b, w_dt, tm, LANES).reshape(B, L, LANES)[..., :n_dt]

    r, v, kk, lw, b2, ke, g, bv = _rwkv_prep(proj_rw, p, lc, tt)
    yf, ybw = _rwkv_scan(r, v, kk, lw, b2, ke, lc, chunks, 2)
    xbc = _ssm_conv(proj_zx, p['ssm_conv_w'], p['ssm_conv_b'], lc, tt, D, 2 * D)
    y2 = _ssd_scan(xbc, proj_dt, p, lc)
    yssm = _from_cm(_ssm_readout(y2, xbc, proj_zx, p, lc, tt), rows)

    x1, h2, aff = _merge(yf, ybw, bv, g, yssm, gates, x, mods, p, norm1_post[l], norm2_pre[l], router[l], lc, tt)
    slot, offs = _select(aff, cap)
    moe = _moe(h2, slot, offs, aff, exp_w1[l].astype(BF16), exp_w3[l].astype(BF16), exp_w2[l].astype(BF16), cap)
    return _final(x1, moe, mods[:, 5:6], norm2_post[l], tt)
```

```python
import functools
import math

import jax
import jax.numpy as jnp
from jax import lax
from jax.experimental import pallas as pl
from jax.experimental.pallas import tpu as pltpu

F32 = jnp.float32
BF16 = jnp.bfloat16

GRID_W = 64
NORM_EPS = 1e-6
RW_GN_EPS = 64e-5
HEAD = 64
LANES = 128
HALO = 16
RW_CHUNK = 64
RW_LORA = 3 * LANES
SSM_TAPS = 5
SSD_CHUNK = 128
SSM_HEADS, SSM_GROUPS, SSM_STATE = 32, 4, 128
VMEM_LIMIT = 48 * 1024 * 1024
MOE_VMEM_LIMIT = 56 * 1024 * 1024


def _dot(a, b):
    return jnp.dot(a.astype(BF16), b.astype(BF16), preferred_element_type=F32)


def _dot_nt(a, b):
    return lax.dot_general(a.astype(BF16), b.astype(BF16), (((1,), (1,)), ((), ())), preferred_element_type=F32)


def _split3(x):
    hi = x.astype(BF16)
    r1 = x - hi.astype(F32)
    mid = r1.astype(BF16)
    lo = (r1 - mid.astype(F32)).astype(BF16)
    return hi, mid, lo


def _dot_exact_lhs(m, x, terms=3):
    mb = m.astype(BF16)
    return sum(jnp.dot(mb, t, preferred_element_type=F32) for t in _split3(x)[:terms])


def _sigmoid(x):
    return 1.0 / (1.0 + jnp.exp(-x))


def _softplus(x):
    return jnp.maximum(x, 0.0) + jnp.log(1.0 + jnp.exp(-jnp.abs(x)))


def _head_sum(x, ones_bd):
    hi = x.astype(BF16)
    lo = (x - hi.astype(F32)).astype(BF16)
    ob = ones_bd.astype(BF16)
    n = x.shape[1] // LANES
    part = lambda t, i: jnp.dot(t[:, i * LANES:(i + 1) * LANES], ob, preferred_element_type=F32)
    return jnp.concatenate([part(hi, i) + part(lo, i) for i in range(n)], axis=1)


def _head_ones():
    i = jnp.arange(LANES)
    return ((i[:, None] // HEAD) == (i[None, :] // HEAD)).astype(F32)


def _halo_specs(tt, width, col, L):
    per = tt // HALO
    nh = L // HALO
    main = pl.BlockSpec((None, tt, width), lambda b, i, *_: (b, i, col(*_)))
    prev = pl.BlockSpec((None, HALO, width), lambda b, i, *_: (b, jnp.maximum(i * per - 1, 0), col(*_)))
    nxt = pl.BlockSpec((None, HALO, width), lambda b, i, *_: (b, jnp.minimum((i + 1) * per, nh - 1), col(*_)))
    return main, prev, nxt


def _seg_edges(i, tt, lc, L):
    t0 = i * tt
    has_prev = jnp.logical_and(t0 != 0, t0 != lc)
    has_next = jnp.logical_and(t0 + tt != lc, t0 + tt != L)
    return has_prev.astype(F32), has_next.astype(F32)


def _adaln_kernel(c_ref, w_ref, b_ref, o_ref):
    c = c_ref[...]
    o_ref[...] = jnp.dot(c * _sigmoid(c), w_ref[...], preferred_element_type=F32,
                         precision=lax.Precision.HIGHEST) + b_ref[...]


def _adaln(cond, w, b, tn):
    M, D = cond.shape
    N = w.shape[1]
    return pl.pallas_call(
        _adaln_kernel,
        grid=(N // tn,),
        in_specs=[pl.BlockSpec((M, D), lambda j: (0, 0)), pl.BlockSpec((D, tn), lambda j: (0, j)),
                  pl.BlockSpec((1, tn), lambda j: (0, j))],
        out_specs=pl.BlockSpec((M, tn), lambda j: (0, j)),
        out_shape=jax.ShapeDtypeStruct((M, N), F32),
        compiler_params=pltpu.CompilerParams(dimension_semantics=("parallel",), vmem_limit_bytes=VMEM_LIMIT),
        name="adaln",
    )(cond, w, b.reshape(1, N))


def _prenorm_kernel(x_ref, mod_ref, nw_ref, o_ref):
    x = x_ref[...]
    y = x * lax.rsqrt(jnp.mean(x * x, axis=-1, keepdims=True) + NORM_EPS) * nw_ref[...]
    o_ref[...] = (y * (1.0 + mod_ref[1:2, :]) + mod_ref[0:1, :]).astype(o_ref.dtype)


def _prenorm(x, mods, nw, tm):
    B, N, D = x.shape
    per_sample = mods.shape[0] == B
    tok = pl.BlockSpec((None, tm, D), lambda b, i: (b, i, 0))
    return pl.pallas_call(
        _prenorm_kernel,
        grid=(B, N // tm),
        in_specs=[tok, pl.BlockSpec((None, 6, D), (lambda b, i: (b, 0, 0)) if per_sample else (lambda b, i: (0, 0, 0))),
                  pl.BlockSpec((1, D), lambda b, i: (0, 0))],
        out_specs=tok,
        out_shape=jax.ShapeDtypeStruct((B, N, D), BF16),
        compiler_params=pltpu.CompilerParams(dimension_semantics=("parallel", "parallel"),
                                             vmem_limit_bytes=VMEM_LIMIT),
        name="prenorm",
    )(x, mods, nw.reshape(1, D))


def _matmul_kernel(a_ref, w_ref, o_ref):
    o_ref[...] = jnp.dot(a_ref[...], w_ref[...], preferred_element_type=F32).astype(o_ref.dtype)


def _matmul(a, w, tm, tn, out_dtype=F32):
    M, K = a.shape
    N = w.shape[1]
    assert M % tm == 0 and N % tn == 0
    return pl.pallas_call(
        _matmul_kernel,
        grid=(N // tn, M // tm),
        in_specs=[pl.BlockSpec((tm, K), lambda j, i: (i, 0)), pl.BlockSpec((K, tn), lambda j, i: (0, j))],
        out_specs=pl.BlockSpec((tm, tn), lambda j, i: (i, j)),
        out_shape=jax.ShapeDtypeStruct((M, N), out_dtype),
        compiler_params=pltpu.CompilerParams(dimension_semantics=("parallel", "parallel"),
                                             vmem_limit_bytes=VMEM_LIMIT),
        name="matmul",
    )(a, w)


def _rwkv_prep_kernel(p_ref, pp_ref, pn_ref, mu_ref, w2_ref, a2_ref, g2_ref, w0_ref, a0_ref, kkw_ref, ka_ref,
                      rk_ref, ones_ref, r_ref, v_ref, kk_ref, lw_ref, b_ref, ke_ref, g_ref, bv_ref, *, tt, lc, L):
    has_prev, has_next = _seg_edges(pl.program_id(1), tt, lc, L)
    rows = lax.broadcasted_iota(jnp.int32, (tt, 1), 0)
    first, last = rows == 0, rows == tt - 1

    def shifted(lo, hi):
        u = p_ref[:, lo:hi].astype(F32)
        pv = pp_ref[HALO - 1:HALO, lo:hi].astype(F32) * has_prev
        nx = pn_ref[0:1, lo:hi].astype(F32) * has_next
        up = jnp.where(first, pv, pltpu.roll(u, 1, 0))
        un = jnp.where(last, nx, pltpu.roll(u, tt - 1, 0))
        return u + mu_ref[0:1, lo:hi] * (up - u) + mu_ref[1:2, lo:hi] * (un - u)

    C = r_ref.shape[-1]
    lane = lax.broadcasted_iota(jnp.int32, (1, LANES), 1)
    half = [(lane < HEAD).astype(F32), (lane >= HEAD).astype(F32)]
    xw = jnp.tanh(shifted(3 * C, 3 * C + LANES))
    xa = shifted(3 * C + LANES, 3 * C + 2 * LANES)
    xg = _sigmoid(shifted(3 * C + 2 * LANES, 3 * C + 3 * LANES))
    g_ref[...] = _dot(xg, g2_ref[...]).astype(g_ref.dtype)
    r = shifted(0, C)
    k = shifted(C, 2 * C)
    v = shifted(2 * C, 3 * C)
    r_ref[...] = r.astype(r_ref.dtype)
    v_ref[...] = v.astype(v_ref.dtype)
    ones_bd = ones_ref[...]
    kkr = k * kkw_ref[...]
    kk = kkr * lax.rsqrt(jnp.maximum(_head_sum(kkr * kkr, ones_bd), 1e-12))
    kk_ref[...] = kk.astype(kk_ref.dtype)
    ksum = jnp.zeros_like(k)
    for d in (0, 1):
        w_raw = w0_ref[d:d + 1, :] + _dot(xw * half[d], w2_ref[...])
        lw_ref[d] = -math.exp(-0.5) * _sigmoid(w_raw)
        a = _sigmoid(a0_ref[d:d + 1, :] + _dot(xa * half[d], a2_ref[...]))
        b_ref[d] = (kk * a).astype(b_ref.dtype)
        ke = k * (1.0 + (a - 1.0) * ka_ref[...])
        ke_ref[d] = ke.astype(ke_ref.dtype)
        ksum = ksum + ke
    bv_ref[...] = (_head_sum(r * ksum * rk_ref[...], ones_bd) * v).astype(bv_ref.dtype)


def _rwkv_prep(proj, p, lc, tt):
    B, L, W = proj.shape
    C = (W - RW_LORA) // 3
    assert L % tt == 0 and lc % tt == 0
    main, prev, nxt = _halo_specs(tt, W, lambda: 0, L)
    full = lambda a: pl.BlockSpec(a.shape, lambda b, i: (0,) * a.ndim)
    w2 = p['rw_w2'].reshape(2 * HEAD, C)
    a2 = p['rw_a2'].reshape(2 * HEAD, C)
    params = [p['rw_mu'], w2, a2, p['rw_g2'], p['rw_w0'], p['rw_a0'], p['rw_kk'].reshape(1, C),
              p['rw_ka'].reshape(1, C), p['rw_rk'].reshape(1, C), _head_ones()]
    one = pl.BlockSpec((None, tt, C), lambda b, i: (b, i, 0))
    two = pl.BlockSpec((2, None, tt, C), lambda b, i: (0, b, i, 0))
    s1 = jax.ShapeDtypeStruct((B, L, C), BF16)
    s2 = jax.ShapeDtypeStruct((2, B, L, C), BF16)
    return pl.pallas_call(
        functools.partial(_rwkv_prep_kernel, tt=tt, lc=lc, L=L),
        grid=(B, L // tt),
        in_specs=[main, prev, nxt] + [full(a) for a in params],
        out_specs=[one, one, one, two, two, two, one, one],
        out_shape=[s1, s1, s1, jax.ShapeDtypeStruct((2, B, L, C), F32), s2, s2, s1, s1],
        compiler_params=pltpu.CompilerParams(dimension_semantics=("parallel", "parallel"),
                                             vmem_limit_bytes=VMEM_LIMIT),
        name="rwkv_prep",
    )(proj, proj, proj, *params)


def _rwkv_masks():
    T = RW_CHUNK
    out = []
    for rev in (False, True):
        t = jnp.arange(T)
        t = (T - 1 - t) if rev else t
        rt, ct = t[:, None], jnp.tile(t, 2)[None, :]
        ms = [rt > ct, rt >= ct]
        s = 1
        while s < T:
            br, bc = rt // s, ct // s
            ms.append((br // 2 == bc // 2) & (br % 2 == 1) & (bc % 2 == 0))
            s *= 2
        out.append(jnp.stack(ms))
    return jnp.stack(out).astype(F32)


def _rwkv_scan_kernel(m_ref, rf_ref, vf_ref, kkf_ref, rb_ref, vb_ref, kkb_ref, lwf_ref, bf_ref, kf_ref,
                      lwb_ref, bb_ref, kb_ref, yf_ref, yb_ref, st_ref, *, chunks, pairs):
    T = RW_CHUNK

    @pl.when(pl.program_id(2) == 0)
    def _():
        st_ref[...] = jnp.zeros_like(st_ref)

    row = lax.broadcasted_iota(jnp.int32, (2 * T, 2 * T), 0)
    col = lax.broadcasted_iota(jnp.int32, (2 * T, 2 * T), 1)
    bd = ((row >> 6) == (col >> 6)).astype(F32)
    trow = lax.broadcasted_iota(jnp.int32, (T, LANES), 0)
    tcol = lax.broadcasted_iota(jnp.int32, (T, LANES), 1)
    eye = (trow == (tcol & (T - 1))).astype(F32)
    h0f = (tcol < HEAD).astype(F32)
    h1f = 1.0 - h0f
    bdize = lambda z: jnp.concatenate([z * h0f, z * h1f], axis=0)
    refs = ((rf_ref, vf_ref, kkf_ref, lwf_ref, bf_ref, kf_ref, yf_ref),
            (rb_ref, vb_ref, kkb_ref, lwb_ref, bb_ref, kb_ref, yb_ref))
    chains = [(d, q) for d in (0, 1) for q in range(pairs)]
    order = {0: list(range(chunks)), 1: list(range(chunks - 1, -1, -1))}
    units = [(d, q, j) for j in range(chunks) for (d, q) in chains]
    n_levels = RW_CHUNK.bit_length() - 1

    def tile(ref, d, q, j):
        cj = order[d][j]
        return ref[cj * T:(cj + 1) * T, q * LANES:(q + 1) * LANES].astype(F32)

    A = {}
    for un in units:
        d, q, j = un
        r_ref, v_ref, kk_ref, lw_ref, b_ref, k_ref, _ = refs[d]
        lw = tile(lw_ref, d, q, j)
        c = _dot_exact_lhs(m_ref[d, 1, :, :T], lw, terms=2)
        gl = jnp.exp(jnp.sum(lw, axis=0, keepdims=True))
        e_nc = jnp.exp(-c)
        at = -tile(kk_ref, d, q, j) * jnp.exp(c - lw)
        rtl = tile(r_ref, d, q, j) * jnp.exp(c)
        bt = tile(b_ref, d, q, j) * e_nc
        kt = tile(k_ref, d, q, j) * e_nc
        A[un] = dict(gl=gl, ar=jnp.concatenate([at, rtl], axis=0).astype(BF16),
                     bk=jnp.concatenate([bt * gl, kt * gl], axis=0).astype(BF16),
                     rhs=jnp.concatenate([bdize(bt), bdize(kt)], axis=0).astype(BF16))
    for un in units:
        a = A[un]
        a['aa'] = lax.dot_general(a['ar'], a.pop('rhs'), (((1,), (1,)), ((), ())),
                                  preferred_element_type=F32)
    for un in units:
        d = un[0]
        a = A[un]
        aa = a.pop('aa')
        a['n'] = aa[:T, :2 * T] * m_ref[d, 0]
        a['ak'] = (aa[:T, 2 * T:] * m_ref[d, 0]).astype(BF16)
        a['rbrk'] = jnp.concatenate([aa[T:, :2 * T] * m_ref[d, 1], aa[T:, 2 * T:] * m_ref[d, 1]], axis=1).astype(BF16)
        a['p'] = eye + a['n'] * m_ref[d, 2]
    for lv in range(1, n_levels):
        for un in units:
            a = A[un]
            a['pn'] = _dot(a['p'], bdize(a['n'] * m_ref[un[0], 2 + lv]))
        for un in units:
            a = A[un]
            a['p'] = a['p'] + _dot(a.pop('pn'), bdize(a['p']))
    for un in units:
        d, q, j = un
        a = A[un]
        a['vbd'] = bdize(tile(refs[d][1], d, q, j)).astype(BF16)
        a['akv'] = jnp.dot(a.pop('ak'), a['vbd'], preferred_element_type=F32)
        a['p'] = a['p'].astype(BF16)
        a.pop('n')

    st = {ch: st_ref[ci] for ci, ch in enumerate(chains)}
    for j in range(chunks):
        x = {}
        for ch in chains:
            x[ch] = _dot_nt(A[ch + (j,)]['ar'], st[ch])
        u = {}
        for ch in chains:
            a = A[ch + (j,)]
            u[ch] = _dot(a['p'], bdize(x[ch][:T] + a['akv']))
        for ch in chains:
            d, q = ch
            a = A[ch + (j,)]
            y = x[ch][T:] + jnp.dot(a['rbrk'], jnp.concatenate([bdize(u[ch]).astype(BF16), a['vbd']], axis=0),
                                    preferred_element_type=F32)
            cj = order[d][j]
            refs[d][6][cj * T:(cj + 1) * T, q * LANES:(q + 1) * LANES] = y.astype(yf_ref.dtype)
            uv = jnp.concatenate([u[ch], tile(refs[d][1], d, q, j)], axis=0)
            st[ch] = (st[ch] * a['gl'] + _dot(uv.T, a['bk'])) * bd
    for ci, ch in enumerate(chains):
        st_ref[ci] = st[ch]


def _rwkv_scan(r, v, kk, lw, b2, k2, lc, chunks, pairs):
    B, L, C = r.shape
    tb = chunks * RW_CHUNK
    wl = pairs * LANES
    assert L % tb == 0 and lc % tb == 0 and C % wl == 0
    nblk, nctx = L // tb, lc // tb

    def bwd(i):
        return jnp.where(i < nctx, nctx - 1 - i, nblk - 1 - (i - nctx))

    sf = pl.BlockSpec((None, tb, wl), lambda b, p, i: (b, i, p))
    sb = pl.BlockSpec((None, tb, wl), lambda b, p, i: (b, bwd(i), p))
    df = pl.BlockSpec((None, None, tb, wl), lambda b, p, i: (0, b, i, p))
    db = pl.BlockSpec((None, None, tb, wl), lambda b, p, i: (1, b, bwd(i), p))
    masks = _rwkv_masks()
    return pl.pallas_call(
        functools.partial(_rwkv_scan_kernel, chunks=chunks, pairs=pairs),
        grid=(B, C // wl, nblk),
        in_specs=[pl.BlockSpec(masks.shape, lambda b, p, i: (0, 0, 0, 0)),
                  sf, sf, sf, sb, sb, sb, df, df, df, db, db, db],
        out_specs=[sf, sb],
        out_shape=[jax.ShapeDtypeStruct((B, L, C), BF16)] * 2,
        scratch_shapes=[pltpu.VMEM((2 * pairs, LANES, LANES), F32)],
        compiler_params=pltpu.CompilerParams(
            dimension_semantics=("parallel", "parallel", "arbitrary"),
            vmem_limit_bytes=VMEM_LIMIT),
        name="rwkv_scan",
    )(masks, r, v, kk, r, v, kk, lw, b2, k2, lw, b2, k2)


def _ssm_conv_kernel(x_ref, xp_ref, xn_ref, w_ref, b_ref, o_ref, *, tt, lc, L):
    has_prev, has_next = _seg_edges(pl.program_id(1), tt, lc, L)
    ext = jnp.concatenate([xp_ref[...].astype(F32) * has_prev, x_ref[...].astype(F32),
                           xn_ref[...].astype(F32) * has_next], axis=0)
    n = tt + 2 * HALO
    acc = jnp.zeros(x_ref.shape, F32) + b_ref[...]
    for j in range(SSM_TAPS):
        off = j - SSM_TAPS // 2
        sh = ext if off == 0 else pltpu.roll(ext, (-off) % n, 0)
        acc = acc + sh[HALO:HALO + tt] * w_ref[j:j + 1, :]
    o_ref[...] = (acc * _sigmoid(acc)).astype(o_ref.dtype)


def _ssm_conv(zx, w, b, lc, tt, tw, col0):
    B, L, _ = zx.shape
    cc = w.shape[0]
    assert cc % tw == 0 and col0 % tw == 0 and L % tt == 0 and lc % tt == 0
    main, prev, nxt = _halo_specs(tt, tw, lambda c: c + col0 // tw, L)
    wt = jnp.transpose(w)
    return pl.pallas_call(
        functools.partial(_ssm_conv_kernel, tt=tt, lc=lc, L=L),
        grid=(B, L // tt, cc // tw),
        in_specs=[main, prev, nxt, pl.BlockSpec((SSM_TAPS, tw), lambda b, i, c: (0, c)),
                  pl.BlockSpec((1, tw), lambda b, i, c: (0, c))],
        out_specs=pl.BlockSpec((None, tt, tw), lambda b, i, c: (b, i, c)),
        out_shape=jax.ShapeDtypeStruct((B, L, cc), BF16),
        compiler_params=pltpu.CompilerParams(dimension_semantics=("parallel", "parallel", "parallel"),
                                             vmem_limit_bytes=VMEM_LIMIT),
        name="ssm_conv",
    )(zx, zx, zx, wt, b.reshape(1, cc))


def _ssd_scan_kernel(x_ref, bm_ref, cm_ref, dt_ref, dtt_ref, a_ref, at_ref, bias_ref, biast_ref, y_ref,
                     ht_ref, *, groups, hpg):
    Q = SSD_CHUNK
    rev = pl.program_id(1) == 1

    @pl.when(pl.program_id(2) == 0)
    def _():
        ht_ref[...] = jnp.zeros_like(ht_ref)

    row = lax.broadcasted_iota(jnp.int32, (Q, Q), 0)
    col = lax.broadcasted_iota(jnp.int32, (Q, Q), 1)
    incl = jnp.where(rev, col - row, row - col) >= 0
    tri = incl.astype(F32)
    dt = _softplus(dt_ref[...] + bias_ref[...])
    cum = _dot_exact_lhs(tri, dt * a_ref[...])
    dtt = _softplus(dtt_ref[...] + biast_ref[...])
    hi, mid, lo = _split3(dtt * at_ref[...])
    trib = tri.astype(BF16)
    nt = lambda a_, b_: lax.dot_general(a_, b_, (((1,), (1,)), ((), ())), preferred_element_type=F32)
    cumt = nt(hi, trib) + nt(mid, trib) + nt(lo, trib)
    head0 = lax.broadcasted_iota(jnp.int32, (1, LANES), 1) < HEAD
    h0f = head0.astype(F32)
    h1f = 1.0 - h0f
    for g in range(groups):
        bm = bm_ref[:, g * SSM_STATE:(g + 1) * SSM_STATE]
        cm = cm_ref[:, g * SSM_STATE:(g + 1) * SSM_STATE]
        cb = _dot_nt(cm, bm)
        bmt = bm.astype(F32).T
        for pp in range(hpg // 2):
            e0 = g * hpg + 2 * pp
            lanes = slice((e0 // 2) * LANES, (e0 // 2 + 1) * LANES)
            cum_l = jnp.where(head0, cum[:, e0:e0 + 1], cum[:, e0 + 1:e0 + 2])
            dt_l = jnp.where(head0, dt[:, e0:e0 + 1], dt[:, e0 + 1:e0 + 2])
            ms = []
            for e in (e0, e0 + 1):
                seg = jnp.exp(jnp.where(incl, cum[:, e:e + 1] - cumt[e:e + 1, :], -jnp.inf))
                ms.append(cb * seg)
            xdt = x_ref[:, lanes].astype(F32) * dt_l
            clast = jnp.min(cum_l, axis=0, keepdims=True)
            htp = ht_ref[:, lanes]
            y = _dot(jnp.concatenate(ms, axis=1), jnp.concatenate([xdt * h0f, xdt * h1f], axis=0))
            y_ref[:, lanes] = (y + _dot(cm, htp) * jnp.exp(cum_l)).astype(y_ref.dtype)
            ht_ref[:, lanes] = htp * jnp.exp(clast) + _dot(bmt, xdt * jnp.exp(clast - cum_l))


def _ssd_scan(xbc, dtp, p, lc):
    B, L, _ = xbc.shape
    Q = SSD_CHUNK
    H, G, N = SSM_HEADS, SSM_GROUPS, SSM_STATE
    xw = H * HEAD
    gw = G * N
    assert L % Q == 0 and lc % Q == 0 and xw % gw == 0
    nck, nctx = L // Q, lc // Q
    dt = jnp.pad(jnp.moveaxis(dtp.reshape(B, L, 2, H), 2, 1), ((0, 0), (0, 0), (0, 0), (0, LANES - H)))
    dtt = jnp.swapaxes(dt, 2, 3)
    a = jnp.pad(-jnp.exp(p['ssm_a_log']), ((0, 0), (0, LANES - H)))
    bias = jnp.pad(p['ssm_dt_bias'], ((0, 0), (0, LANES - H)))

    def ck(d, i):
        bwd = jnp.where(i < nctx, nctx - 1 - i, nck - 1 - (i - nctx))
        return jnp.where(d == 0, i, bwd)

    return pl.pallas_call(
        functools.partial(_ssd_scan_kernel, groups=G, hpg=H // G),
        grid=(B, 2, nck),
        in_specs=[pl.BlockSpec((None, Q, xw), lambda b, d, i: (b, ck(d, i), 0)),
                  pl.BlockSpec((None, Q, gw), lambda b, d, i: (b, ck(d, i), xw // gw)),
                  pl.BlockSpec((None, Q, gw), lambda b, d, i: (b, ck(d, i), xw // gw + 1)),
                  pl.BlockSpec((None, None, Q, LANES), lambda b, d, i: (b, d, ck(d, i), 0)),
                  pl.BlockSpec((None, None, LANES, Q), lambda b, d, i: (b, d, 0, ck(d, i))),
                  pl.BlockSpec((None, 1, LANES), lambda b, d, i: (d, 0, 0)),
                  pl.BlockSpec((None, LANES, 1), lambda b, d, i: (d, 0, 0)),
                  pl.BlockSpec((None, 1, LANES), lambda b, d, i: (d, 0, 0)),
                  pl.BlockSpec((None, LANES, 1), lambda b, d, i: (d, 0, 0))],
        out_specs=pl.BlockSpec((None, None, Q, xw), lambda b, d, i: (d, b, ck(d, i), 0)),
        out_shape=jax.ShapeDtypeStruct((2, B, L, xw), BF16),
        scratch_shapes=[pltpu.VMEM((N, xw), F32)],
        compiler_params=pltpu.CompilerParams(
            dimension_semantics=("parallel", "arbitrary", "arbitrary"),
            vmem_limit_bytes=VMEM_LIMIT),
        name="ssd_scan",
    )(xbc, xbc, xbc, dt, dtt, a.reshape(2, 1, LANES), a.reshape(2, LANES, 1), bias.reshape(2, 1, LANES),
      bias.reshape(2, LANES, 1))


def _ssm_readout_kernel(yf_ref, yb_ref, x_ref, z_ref, d_ref, nw_ref, o_ref):
    z = z_ref[...].astype(F32)
    y = (yf_ref[...].astype(F32) + yb_ref[...].astype(F32) + d_ref[...] * x_ref[...].astype(F32)) * (z * _sigmoid(z))
    o_ref[...] = (y * lax.rsqrt(jnp.mean(y * y, axis=-1, keepdims=True) + NORM_EPS) * nw_ref[...]).astype(o_ref.dtype)


def _ssm_readout(y2, xbc, zx, p, lc, tt):
    _, B, L, C = y2.shape
    S = L - lc
    off = lc // tt
    d_l = jnp.repeat(p['ssm_d'], HEAD).reshape(1, C)
    row = lambda b, i: (b, i + off, 0)
    return pl.pallas_call(
        _ssm_readout_kernel,
        grid=(B, S // tt),
        in_specs=[pl.BlockSpec((None, None, tt, C), lambda b, i: (0, b, i + off, 0)),
                  pl.BlockSpec((None, None, tt, C), lambda b, i: (1, b, i + off, 0)),
                  pl.BlockSpec((None, tt, C), row), pl.BlockSpec((None, tt, C), row),
                  pl.BlockSpec((1, C), lambda b, i: (0, 0)), pl.BlockSpec((1, C), lambda b, i: (0, 0))],
        out_specs=pl.BlockSpec((None, tt, C), lambda b, i: (b, i, 0)),
        out_shape=jax.ShapeDtypeStruct((B, S, C), BF16),
        compiler_params=pltpu.CompilerParams(dimension_semantics=("parallel", "parallel"),
                                             vmem_limit_bytes=VMEM_LIMIT),
        name="ssm_readout",
    )(y2, y2, xbc, zx, d_l, p['ssm_norm'].reshape(1, C))


def _merge_kernel(yf_ref, yb_ref, bv_ref, g_ref, ys_ref, gt_ref, x_ref, mod_ref, lnw_ref, lnb_ref, n1_ref, n2_ref,
                  pa_ref, pb_ref, wo_ref, rt_ref, ones_ref, x1_ref, h2_ref, aff_ref):
    C = yf_ref.shape[-1]
    ones_bd = ones_ref[...]
    y = yf_ref[...].astype(F32) + yb_ref[...].astype(F32)
    mean = _head_sum(y, ones_bd) * (1.0 / HEAD)
    yc = y - mean
    var = _head_sum(yc * yc, ones_bd) * (1.0 / HEAD)
    ya = (yc * lax.rsqrt(var + RW_GN_EPS) * lnw_ref[...] + lnb_ref[...] + bv_ref[...].astype(F32)) * g_ref[...].astype(F32)
    gt = gt_ref[...].astype(F32)
    mix = _sigmoid(gt[:, :C]) * _dot(ya, pa_ref[...]) + _sigmoid(gt[:, C:]) * jnp.dot(
        ys_ref[...], pb_ref[...], preferred_element_type=F32)
    ml = _dot(mix, wo_ref[...])
    rms = lambda t: t * lax.rsqrt(jnp.mean(t * t, axis=-1, keepdims=True) + NORM_EPS)
    x1 = x_ref[...] + mod_ref[2:3, :] * (rms(ml) * n1_ref[...])
    x1_ref[...] = x1
    h2 = (rms(x1) * n2_ref[...] * (1.0 + mod_ref[4:5, :]) + mod_ref[3:4, :]).astype(BF16)
    h2_ref[...] = h2
    logits = lax.dot_general(rt_ref[...], h2, (((1,), (1,)), ((), ())), preferred_element_type=F32)
    e = jnp.exp(logits - jnp.max(logits, axis=0, keepdims=True))
    aff_ref[...] = e / jnp.sum(e, axis=0, keepdims=True)


def _merge(yf, yb, bv, g, yssm, gates, x, mods, p, n1post, n2pre, router, lc, tm):
    B, S, D = x.shape
    C = yf.shape[-1]
    E = router.shape[1]
    off = lc // tm
    lat = pl.BlockSpec((None, tm, C), lambda b, i: (b, i + off, 0))
    tok = lambda w: pl.BlockSpec((None, tm, w), lambda b, i: (b, i, 0))
    full = lambda a: pl.BlockSpec(a.shape, lambda b, i: (0,) * a.ndim, pipeline_mode=pl.Buffered(1))
    row = lambda a: a.reshape(1, -1)
    params = [row(p['rw_ln_w']), row(p['rw_ln_b']), row(n1post), row(n2pre), p['proj_a'].astype(BF16),
              p['proj_b'].astype(BF16), p['w_out'].astype(BF16), jnp.transpose(router).astype(BF16), _head_ones()]
    return pl.pallas_call(
        _merge_kernel,
        grid=(B, S // tm),
        in_specs=[lat, lat, lat, lat, tok(yssm.shape[-1]), tok(2 * C), tok(D),
                  pl.BlockSpec((None, 6, D), lambda b, i: (b, 0, 0))] + [full(a) for a in params],
        out_specs=[tok(D), tok(D), pl.BlockSpec((None, E, tm), lambda b, i: (b, 0, i))],
        out_shape=[jax.ShapeDtypeStruct((B, S, D), F32), jax.ShapeDtypeStruct((B, S, D), BF16),
                   jax.ShapeDtypeStruct((B, E, S), F32)],
        compiler_params=pltpu.CompilerParams(dimension_semantics=("parallel", "parallel"),
                                             vmem_limit_bytes=VMEM_LIMIT),
        name="merge",
    )(yf, yb, bv, g, yssm, gates, x, mods, *params)


def _select_kernel(aff_ref, slot_ref, off_ref, *, cap):
    aff = aff_ref[...]
    E, S = aff.shape
    bits = lax.bitcast_convert_type(aff, jnp.int32)
    count = lambda m: jnp.sum(m.astype(F32), axis=1, keepdims=True)
    thr = jnp.zeros((E, 1), jnp.int32)
    for bit in range(30, -1, -1):
        cand = thr | (1 << bit)
        thr = jnp.where(count(bits >= cand) >= cap, cand, thr)
    gt = bits > thr
    eq = bits == thr
    need = cap - count(gt)

    tl = lax.broadcasted_iota(jnp.int32, (LANES, LANES), 0)
    tc = lax.broadcasted_iota(jnp.int32, (LANES, LANES), 1)
    before = (tl < tc).astype(BF16)

    def prefix(m):
        outs, carry = [], jnp.zeros((E, 1), F32)
        for t in range(S // LANES):
            mt = m[:, t * LANES:(t + 1) * LANES]
            outs.append(jnp.dot(mt.astype(BF16), before, preferred_element_type=F32) + carry)
            carry = carry + jnp.sum(mt, axis=1, keepdims=True)
        return jnp.concatenate(outs, axis=1)

    eqf = eq.astype(F32)
    taken = gt.astype(F32) + eqf * (prefix(eqf) < need).astype(F32)
    slot_ref[...] = jnp.where(taken > 0.5, prefix(taken), -1.0).astype(jnp.int32)
    tok = lax.broadcasted_iota(jnp.int32, (S, LANES), 0)
    tile_start = lax.broadcasted_iota(jnp.int32, (S, LANES), 1) * LANES
    off_ref[...] = jnp.dot(taken.astype(BF16), (tok < tile_start).astype(BF16),
                           preferred_element_type=F32).astype(jnp.int32)


def _select(aff, cap):
    B, E, S = aff.shape
    spec = pl.BlockSpec((None, E, S), lambda b: (b, 0, 0))
    return pl.pallas_call(
        functools.partial(_select_kernel, cap=cap),
        grid=(B,), in_specs=[spec], out_specs=[spec, pl.BlockSpec((None, E, LANES), lambda b: (b, 0, 0))],
        out_shape=[jax.ShapeDtypeStruct((B, E, S), jnp.int32), jax.ShapeDtypeStruct((B, E, LANES), jnp.int32)],
        compiler_params=pltpu.CompilerParams(dimension_semantics=("parallel",), vmem_limit_bytes=VMEM_LIMIT),
        name="ec_select",
    )(aff)


def _moe_kernel(off_ref, h_ref, slot_ref, aff_ref, w1_ref, w3_ref, w2_ref, o_ref, xe_ref, gate_ref, ye_ref,
                *, cap, ts, sb):
    b, e = pl.program_id(0), pl.program_id(1)
    S, D = h_ref.shape
    nt, nsb = S // ts, cap // sb
    base = (b * pl.num_programs(1) + e) * (nt + 1)
    slot_iota = lax.broadcasted_iota(jnp.int32, (sb, ts), 0)

    def pairs(body):
        for t in range(nt):
            lo, hi = off_ref[base + t], off_ref[base + t + 1]
            for k in range(nsb):
                def run(t=t, k=k):
                    tok = slice(t * ts, (t + 1) * ts)
                    hit = (slot_iota + k * sb) == slot_ref[pl.ds(e, 1), tok]
                    body(hit, tok, slice(k * sb, (k + 1) * sb))
                pl.when(jnp.logical_and(lo < (k + 1) * sb, hi > k * sb))(run)

    xe_ref[...] = jnp.zeros_like(xe_ref)
    gate_ref[...] = jnp.zeros_like(gate_ref)

    @pl.when(e == 0)
    def _():
        o_ref[...] = jnp.zeros_like(o_ref)

    def gather(hit, tok, rows):
        gate_ref[rows, :] += jnp.sum(jnp.where(hit, aff_ref[pl.ds(e, 1), tok], 0.0), axis=1, keepdims=True)
        xe_ref[rows, :] += jnp.dot(jnp.where(hit, 1.0, 0.0).astype(BF16), h_ref[tok, :],
                                   preferred_element_type=F32)

    pairs(gather)
    xe = xe_ref[...].astype(BF16)
    a1 = jnp.dot(xe, w1_ref[...], preferred_element_type=F32)
    a3 = jnp.dot(xe, w3_ref[...], preferred_element_type=F32)
    hid = (a1 * _sigmoid(a1)) * a3
    ye_ref[...] = (_dot(hid, w2_ref[...]) * gate_ref[...]).astype(BF16)

    def scatter(hit, tok, rows):
        o_ref[tok, :] += lax.dot_general(jnp.where(hit, 1.0, 0.0).astype(BF16), ye_ref[rows, :],
                                         (((0,), (0,)), ((), ())), preferred_element_type=F32)

    pairs(scatter)


def _moe(h2, slot, offs, aff, w1, w3, w2, cap):
    B, S, D = h2.shape
    E = slot.shape[1]
    F = w1.shape[-1]
    ts, sb = min(S, 1024), min(cap, 2 * LANES)
    nt = S // ts
    bounds = jnp.concatenate([offs[..., ::ts // LANES][..., :nt], jnp.full((B, E, 1), cap, jnp.int32)], -1).reshape(-1)
    once = pl.Buffered(1)
    return pl.pallas_call(
        functools.partial(_moe_kernel, cap=cap, ts=ts, sb=sb),
        grid_spec=pltpu.PrefetchScalarGridSpec(
            num_scalar_prefetch=1,
            grid=(B, E),
            in_specs=[pl.BlockSpec((None, S, D), lambda b, e, o: (b, 0, 0), pipeline_mode=once),
                      pl.BlockSpec((None, E, S), lambda b, e, o: (b, 0, 0), pipeline_mode=once),
                      pl.BlockSpec((None, E, S), lambda b, e, o: (b, 0, 0), pipeline_mode=once),
                      pl.BlockSpec((None, D, F), lambda b, e, o: (e, 0, 0)),
                      pl.BlockSpec((None, D, F), lambda b, e, o: (e, 0, 0)),
                      pl.BlockSpec((None, F, D), lambda b, e, o: (e, 0, 0))],
            out_specs=pl.BlockSpec((None, S, D), lambda b, e, o: (b, 0, 0), pipeline_mode=once),
            scratch_shapes=[pltpu.VMEM((cap, D), F32), pltpu.VMEM((cap, 1), F32), pltpu.VMEM((cap, D), BF16)]),
        out_shape=jax.ShapeDtypeStruct((B, S, D), F32),
        compiler_params=pltpu.CompilerParams(dimension_semantics=("parallel", "arbitrary"),
                                             vmem_limit_bytes=MOE_VMEM_LIMIT),
        name="ec_moe",
    )(bounds, h2, slot, aff, w1, w3, w2)


def _final_kernel(x_ref, m_ref, g_ref, nw_ref, o_ref):
    m = m_ref[...]
    o_ref[...] = x_ref[...] + g_ref[...] * (m * lax.rsqrt(jnp.mean(m * m, axis=-1, keepdims=True) + NORM_EPS)
                                            * nw_ref[...])


def _final(x1, moe, g2, nw, tm):
    B, S, D = x1.shape
    tok = pl.BlockSpec((None, tm, D), lambda b, i: (b, i, 0))
    return pl.pallas_call(
        _final_kernel,
        grid=(B, S // tm),
        in_specs=[tok, tok, pl.BlockSpec((None, 1, D), lambda b, i: (b, 0, 0)), pl.BlockSpec((1, D), lambda b, i: (0, 0))],
        out_specs=tok,
        out_shape=jax.ShapeDtypeStruct((B, S, D), F32),
        compiler_params=pltpu.CompilerParams(dimension_semantics=("parallel", "parallel"),
                                             vmem_limit_bytes=VMEM_LIMIT),
        name="final_norm",
    )(x1, moe, g2, nw.reshape(1, D))


def _to_cm(t, rows):
    b, n, ch = t.shape
    return t.reshape(b, rows, GRID_W, ch).swapaxes(1, 2).reshape(b, n, ch)


def _from_cm(t, rows):
    b, n, ch = t.shape
    return t.reshape(b, GRID_W, rows, ch).swapaxes(1, 2).reshape(b, n, ch)


def kernel(x, c, ctx, c_ctx, ada_w, ada_b, norm1_pre, norm1_post, norm2_pre, norm2_post,
           w_in, rw_mu, rw_w0, rw_w2, rw_a0, rw_a2, rw_g2, rw_kk, rw_ka, rw_rk, rw_ln_w, rw_ln_b,
           ssm_conv_w, ssm_conv_b, ssm_dt_bias, ssm_a_log, ssm_d, ssm_norm, proj_a, proj_b, w_out,
           router, exp_w1, exp_w3, exp_w2):
    B, S, D = x.shape
    lc = ctx.shape[1]
    L = lc + S
    rows = S // GRID_W
    E = router.shape[-1]
    cap = 2 * S // E
    assert ada_w.shape[0] == 1, "single trunk layer"
    l = 0
    p = dict(rw_mu=rw_mu[l], rw_w0=rw_w0[l], rw_w2=rw_w2[l], rw_a0=rw_a0[l], rw_a2=rw_a2[l],
             rw_g2=rw_g2[l], rw_kk=rw_kk[l], rw_ka=rw_ka[l], rw_rk=rw_rk[l], rw_ln_w=rw_ln_w[l], rw_ln_b=rw_ln_b[l],
             ssm_conv_w=ssm_conv_w[l], ssm_conv_b=ssm_conv_b[l], ssm_dt_bias=ssm_dt_bias[l], ssm_a_log=ssm_a_log[l],
             ssm_d=ssm_d[l], ssm_norm=ssm_norm[l], proj_a=proj_a[l], proj_b=proj_b[l], w_out=w_out[l])
    big = lc % 256 == 0
    tt = 256 if big else 128
    chunks = 4 if big else 2
    n_rw = 3 * D + RW_LORA
    n_zx = 2 * D + (2 * D + 2 * SSM_GROUPS * SSM_STATE)
    n_dt = 2 * SSM_HEADS

    cond = jnp.concatenate([c, c_ctx[None], jnp.zeros((2 * HALO - B - 1, D), F32)], 0)
    mods_all = _adaln(cond, ada_w[l], ada_b[l], ada_w.shape[-1] // 4).reshape(2 * HALO, 6, D)
    mods, cmods = mods_all[:B], mods_all[B:B + 1]
    hl = _prenorm(x, mods, norm1_pre[l], tt)
    hc = _prenorm(ctx, cmods, norm1_pre[l], tt)
    seq_a = jnp.concatenate([hc, hl], 1).reshape(B * L, D)
    seq_b = jnp.concatenate([hc, _to_cm(hl, rows)], 1).reshape(B * L, D)
    w = w_in[l].astype(BF16)
    tm = 512 if (B * L) % 512 == 0 else 128
    proj_rw = _matmul(seq_a, w[:, :n_rw], tm, n_rw // 3, BF16).reshape(B, L, n_rw)
    gates = _matmul(hl.reshape(B * S, D), w[:, n_rw + n_zx + n_dt:], tm, D, BF16).reshape(B, S, 2 * D)
    proj_zx = _matmul(seq_b, w[:, n_rw:n_rw + n_zx], tm, D, BF16).reshape(B, L, n_zx)
    w_dt = jnp.pad(w[:, n_rw + n_zx:n_rw + n_zx + n_dt], ((0, 0), (0, LANES - n_dt)))
    proj_dt = _matmul(seq_b, w_dt, tm, LANES).reshape(B, L, LANES)[..., :n_dt]

    r, v, kk, lw, b2, ke, g, bv = _rwkv_prep(proj_rw, p, lc, tt)
    yf, ybw = _rwkv_scan(r, v, kk, lw, b2, ke, lc, chunks, 4)
    xbc = _ssm_conv(proj_zx, p['ssm_conv_w'], p['ssm_conv_b'], lc, tt, D, 2 * D)
    y2 = _ssd_scan(xbc, proj_dt, p, lc)
    yssm = _from_cm(_ssm_readout(y2, xbc, proj_zx, p, lc, tt), rows)

    x1, h2, aff = _merge(yf, ybw, bv, g, yssm, gates, x, mods, p, norm1_post[l], norm2_pre[l], router[l], lc, tt)
    slot, offs = _select(aff, cap)
    moe = _moe(h2, slot, offs, aff, exp_w1[l].astype(BF16), exp_w3[l].astype(BF16), exp_w2[l].astype(BF16), cap)
    return _final(x1, moe, mods[:, 5:6], norm2_post[l], tt)
```

```python
import functools
import math

import jax
import jax.numpy as jnp
from jax import lax
from jax.experimental import pallas as pl
from jax.experimental.pallas import tpu as pltpu

F32 = jnp.float32
BF16 = jnp.bfloat16

GRID_W = 64
NORM_EPS = 1e-6
RW_GN_EPS = 64e-5
HEAD = 64
LANES = 128
HALO = 16
RW_CHUNK = 64
RW_LORA = 3 * LANES
SSM_TAPS = 5
SSD_CHUNK = 128
SSM_HEADS, SSM_GROUPS, SSM_STATE = 32, 4, 128
VMEM_LIMIT = 48 * 1024 * 1024
MOE_VMEM_LIMIT = 56 * 1024 * 1024


def _dot(a, b):
    return jnp.dot(a.astype(BF16), b.astype(BF16), preferred_element_type=F32)


def _dot_nt(a, b):
    return lax.dot_general(a.astype(BF16), b.astype(BF16), (((1,), (1,)), ((), ())), preferred_element_type=F32)


def _split3(x):
    hi = x.astype(BF16)
    r1 = x - hi.astype(F32)
    mid = r1.astype(BF16)
    lo = (r1 - mid.astype(F32)).astype(BF16)
    return hi, mid, lo


def _dot_exact_lhs(m, x, terms=3):
    mb = m.astype(BF16)
    return sum(jnp.dot(mb, t, preferred_element_type=F32) for t in _split3(x)[:terms])


def _sigmoid(x):
    return 1.0 / (1.0 + jnp.exp(-x))


def _softplus(x):
    return jnp.maximum(x, 0.0) + jnp.log(1.0 + jnp.exp(-jnp.abs(x)))


def _head_sum(x, ones_bd):
    hi = x.astype(BF16)
    lo = (x - hi.astype(F32)).astype(BF16)
    ob = ones_bd.astype(BF16)
    n = x.shape[1] // LANES
    part = lambda t, i: jnp.dot(t[:, i * LANES:(i + 1) * LANES], ob, preferred_element_type=F32)
    return jnp.concatenate([part(hi, i) + part(lo, i) for i in range(n)], axis=1)


def _head_ones():
    i = jnp.arange(LANES)
    return ((i[:, None] // HEAD) == (i[None, :] // HEAD)).astype(F32)


def _halo_specs(tt, width, col, L):
    per = tt // HALO
    nh = L // HALO
    main = pl.BlockSpec((None, tt, width), lambda b, i, *_: (b, i, col(*_)))
    prev = pl.BlockSpec((None, HALO, width), lambda b, i, *_: (b, jnp.maximum(i * per - 1, 0), col(*_)))
    nxt = pl.BlockSpec((None, HALO, width), lambda b, i, *_: (b, jnp.minimum((i + 1) * per, nh - 1), col(*_)))
    return main, prev, nxt


def _seg_edges(i, tt, lc, L):
    t0 = i * tt
    has_prev = jnp.logical_and(t0 != 0, t0 != lc)
    has_next = jnp.logical_and(t0 + tt != lc, t0 + tt != L)
    return has_prev.astype(F32), has_next.astype(F32)


def _adaln_kernel(c_ref, w_ref, b_ref, o_ref):
    c = c_ref[...]
    o_ref[...] = jnp.dot(c * _sigmoid(c), w_ref[...], preferred_element_type=F32,
                         precision=lax.Precision.HIGHEST) + b_ref[...]


def _adaln(cond, w, b, tn):
    M, D = cond.shape
    N = w.shape[1]
    return pl.pallas_call(
        _adaln_kernel,
        grid=(N // tn,),
        in_specs=[pl.BlockSpec((M, D), lambda j: (0, 0)), pl.BlockSpec((D, tn), lambda j: (0, j)),
                  pl.BlockSpec((1, tn), lambda j: (0, j))],
        out_specs=pl.BlockSpec((M, tn), lambda j: (0, j)),
        out_shape=jax.ShapeDtypeStruct((M, N), F32),
        compiler_params=pltpu.CompilerParams(dimension_semantics=("parallel",), vmem_limit_bytes=VMEM_LIMIT),
        name="adaln",
    )(cond, w, b.reshape(1, N))


def _prenorm_kernel(x_ref, mod_ref, nw_ref, o_ref):
    x = x_ref[...]
    y = x * lax.rsqrt(jnp.mean(x * x, axis=-1, keepdims=True) + NORM_EPS) * nw_ref[...]
    o_ref[...] = (y * (1.0 + mod_ref[1:2, :]) + mod_ref[0:1, :]).astype(o_ref.dtype)


def _prenorm(x, mods, nw, tm):
    B, N, D = x.shape
    per_sample = mods.shape[0] == B
    tok = pl.BlockSpec((None, tm, D), lambda b, i: (b, i, 0))
    return pl.pallas_call(
        _prenorm_kernel,
        grid=(B, N // tm),
        in_specs=[tok, pl.BlockSpec((None, 6, D), (lambda b, i: (b, 0, 0)) if per_sample else (lambda b, i: (0, 0, 0))),
                  pl.BlockSpec((1, D), lambda b, i: (0, 0))],
        out_specs=tok,
        out_shape=jax.ShapeDtypeStruct((B, N, D), BF16),
        compiler_params=pltpu.CompilerParams(dimension_semantics=("parallel", "parallel"),
                                             vmem_limit_bytes=VMEM_LIMIT),
        name="prenorm",
    )(x, mods, nw.reshape(1, D))


def _matmul_kernel(a_ref, w_ref, o_ref):
    o_ref[...] = jnp.dot(a_ref[...], w_ref[...], preferred_element_type=F32).astype(o_ref.dtype)


def _matmul(a, w, tm, tn, out_dtype=F32):
    M, K = a.shape
    N = w.shape[1]
    assert M % tm == 0 and N % tn == 0
    return pl.pallas_call(
        _matmul_kernel,
        grid=(N // tn, M // tm),
        in_specs=[pl.BlockSpec((tm, K), lambda j, i: (i, 0)), pl.BlockSpec((K, tn), lambda j, i: (0, j))],
        out_specs=pl.BlockSpec((tm, tn), lambda j, i: (i, j)),
        out_shape=jax.ShapeDtypeStruct((M, N), out_dtype),
        compiler_params=pltpu.CompilerParams(dimension_semantics=("parallel", "parallel"),
                                             vmem_limit_bytes=VMEM_LIMIT),
        name="matmul",
    )(a, w)


def _rwkv_prep_kernel(p_ref, pp_ref, pn_ref, mu_ref, w2_ref, a2_ref, g2_ref, w0_ref, a0_ref, kkw_ref, ka_ref,
                      rk_ref, ones_ref, r_ref, v_ref, kk_ref, lw_ref, b_ref, ke_ref, g_ref, bv_ref, *, tt, lc, L):
    has_prev, has_next = _seg_edges(pl.program_id(1), tt, lc, L)
    rows = lax.broadcasted_iota(jnp.int32, (tt, 1), 0)
    first, last = rows == 0, rows == tt - 1

    def shifted(lo, hi):
        u = p_ref[:, lo:hi].astype(F32)
        pv = pp_ref[HALO - 1:HALO, lo:hi].astype(F32) * has_prev
        nx = pn_ref[0:1, lo:hi].astype(F32) * has_next
        up = jnp.where(first, pv, pltpu.roll(u, 1, 0))
        un = jnp.where(last, nx, pltpu.roll(u, tt - 1, 0))
        return u + mu_ref[0:1, lo:hi] * (up - u) + mu_ref[1:2, lo:hi] * (un - u)

    C = r_ref.shape[-1]
    lane = lax.broadcasted_iota(jnp.int32, (1, LANES), 1)
    half = [(lane < HEAD).astype(F32), (lane >= HEAD).astype(F32)]
    xw = jnp.tanh(shifted(3 * C, 3 * C + LANES))
    xa = shifted(3 * C + LANES, 3 * C + 2 * LANES)
    xg = _sigmoid(shifted(3 * C + 2 * LANES, 3 * C + 3 * LANES))
    g_ref[...] = _dot(xg, g2_ref[...]).astype(g_ref.dtype)
    r = shifted(0, C)
    k = shifted(C, 2 * C)
    v = shifted(2 * C, 3 * C)
    r_ref[...] = r.astype(r_ref.dtype)
    v_ref[...] = v.astype(v_ref.dtype)
    ones_bd = ones_ref[...]
    kkr = k * kkw_ref[...]
    kk = kkr * lax.rsqrt(jnp.maximum(_head_sum(kkr * kkr, ones_bd), 1e-12))
    kk_ref[...] = kk.astype(kk_ref.dtype)
    ksum = jnp.zeros_like(k)
    for d in (0, 1):
        w_raw = w0_ref[d:d + 1, :] + _dot(xw * half[d], w2_ref[...])
        lw_ref[d] = -math.exp(-0.5) * _sigmoid(w_raw)
        a = _sigmoid(a0_ref[d:d + 1, :] + _dot(xa * half[d], a2_ref[...]))
        b_ref[d] = (kk * a).astype(b_ref.dtype)
        ke = k * (1.0 + (a - 1.0) * ka_ref[...])
        ke_ref[d] = ke.astype(ke_ref.dtype)
        ksum = ksum + ke
    bv_ref[...] = (_head_sum(r * ksum * rk_ref[...], ones_bd) * v).astype(bv_ref.dtype)


def _rwkv_prep(proj, p, lc, tt):
    B, L, W = proj.shape
    C = (W - RW_LORA) // 3
    assert L % tt == 0 and lc % tt == 0
    main, prev, nxt = _halo_specs(tt, W, lambda: 0, L)
    full = lambda a: pl.BlockSpec(a.shape, lambda b, i: (0,) * a.ndim)
    w2 = p['rw_w2'].reshape(2 * HEAD, C)
    a2 = p['rw_a2'].reshape(2 * HEAD, C)
    params = [p['rw_mu'], w2, a2, p['rw_g2'], p['rw_w0'], p['rw_a0'], p['rw_kk'].reshape(1, C),
              p['rw_ka'].reshape(1, C), p['rw_rk'].reshape(1, C), _head_ones()]
    one = pl.BlockSpec((None, tt, C), lambda b, i: (b, i, 0))
    two = pl.BlockSpec((2, None, tt, C), lambda b, i: (0, b, i, 0))
    s1 = jax.ShapeDtypeStruct((B, L, C), BF16)
    s2 = jax.ShapeDtypeStruct((2, B, L, C), BF16)
    return pl.pallas_call(
        functools.partial(_rwkv_prep_kernel, tt=tt, lc=lc, L=L),
        grid=(B, L // tt),
        in_specs=[main, prev, nxt] + [full(a) for a in params],
        out_specs=[one, one, one, two, two, two, one, one],
        out_shape=[s1, s1, s1, jax.ShapeDtypeStruct((2, B, L, C), F32), s2, s2, s1, s1],
        compiler_params=pltpu.CompilerParams(dimension_semantics=("parallel", "parallel"),
                                             vmem_limit_bytes=VMEM_LIMIT),
        name="rwkv_prep",
    )(proj, proj, proj, *params)


def _rwkv_masks():
    T = RW_CHUNK
    out = []
    for rev in (False, True):
        t = jnp.arange(T)
        t = (T - 1 - t) if rev else t
        rt, ct = t[:, None], jnp.tile(t, 2)[None, :]
        ms = [rt > ct, rt >= ct]
        s = 1
        while s < T:
            br, bc = rt // s, ct // s
            ms.append((br // 2 == bc // 2) & (br % 2 == 1) & (bc % 2 == 0))
            s *= 2
        out.append(jnp.stack(ms))
    return jnp.stack(out).astype(F32)


def _rwkv_scan_kernel(m_ref, rf_ref, vf_ref, kkf_ref, rb_ref, vb_ref, kkb_ref, lwf_ref, bf_ref, kf_ref,
                      lwb_ref, bb_ref, kb_ref, yf_ref, yb_ref, st_ref, *, chunks, pairs):
    T = RW_CHUNK

    @pl.when(pl.program_id(2) == 0)
    def _():
        st_ref[...] = jnp.zeros_like(st_ref)

    row = lax.broadcasted_iota(jnp.int32, (2 * T, 2 * T), 0)
    col = lax.broadcasted_iota(jnp.int32, (2 * T, 2 * T), 1)
    bd = ((row >> 6) == (col >> 6)).astype(F32)
    trow = lax.broadcasted_iota(jnp.int32, (T, LANES), 0)
    tcol = lax.broadcasted_iota(jnp.int32, (T, LANES), 1)
    eye = (trow == (tcol & (T - 1))).astype(F32)
    h0f = (tcol < HEAD).astype(F32)
    h1f = 1.0 - h0f
    bdize = lambda z: jnp.concatenate([z * h0f, z * h1f], axis=0)
    refs = ((rf_ref, vf_ref, kkf_ref, lwf_ref, bf_ref, kf_ref, yf_ref),
            (rb_ref, vb_ref, kkb_ref, lwb_ref, bb_ref, kb_ref, yb_ref))
    chains = [(d, q) for d in (0, 1) for q in range(pairs)]
    order = {0: list(range(chunks)), 1: list(range(chunks - 1, -1, -1))}
    units = [(d, q, j) for j in range(chunks) for (d, q) in chains]
    n_levels = RW_CHUNK.bit_length() - 1

    def tile(ref, d, q, j):
        cj = order[d][j]
        return ref[cj * T:(cj + 1) * T, q * LANES:(q + 1) * LANES].astype(F32)

    A = {}
    for un in units:
        d, q, j = un
        r_ref, v_ref, kk_ref, lw_ref, b_ref, k_ref, _ = refs[d]
        lw = tile(lw_ref, d, q, j)
        c = _dot_exact_lhs(m_ref[d, 1, :, :T], lw, terms=2)
        gl = jnp.exp(jnp.sum(lw, axis=0, keepdims=True))
        e_nc = jnp.exp(-c)
        at = -tile(kk_ref, d, q, j) * jnp.exp(c - lw)
        rtl = tile(r_ref, d, q, j) * jnp.exp(c)
        bt = tile(b_ref, d, q, j) * e_nc
        kt = tile(k_ref, d, q, j) * e_nc
        A[un] = dict(gl=gl, ar=jnp.concatenate([at, rtl], axis=0).astype(BF16),
                     bk=jnp.concatenate([bt * gl, kt * gl], axis=0).astype(BF16),
                     rhs=jnp.concatenate([bdize(bt), bdize(kt)], axis=0).astype(BF16))
    for un in units:
        a = A[un]
        a['aa'] = lax.dot_general(a['ar'], a.pop('rhs'), (((1,), (1,)), ((), ())),
                                  preferred_element_type=F32)
    for un in units:
        d = un[0]
        a = A[un]
        aa = a.pop('aa')
        a['n'] = aa[:T, :2 * T] * m_ref[d, 0]
        a['ak'] = (aa[:T, 2 * T:] * m_ref[d, 0]).astype(BF16)
        a['rbrk'] = jnp.concatenate([aa[T:, :2 * T] * m_ref[d, 1], aa[T:, 2 * T:] * m_ref[d, 1]], axis=1).astype(BF16)
        a['p'] = eye + a['n'] * m_ref[d, 2]
    for lv in range(1, n_levels):
        for un in units:
            a = A[un]
            a['pn'] = _dot(a['p'], bdize(a['n'] * m_ref[un[0], 2 + lv]))
        for un in units:
            a = A[un]
            a['p'] = a['p'] + _dot(a.pop('pn'), bdize(a['p']))
    for un in units:
        d, q, j = un
        a = A[un]
        a['vbd'] = bdize(tile(refs[d][1], d, q, j)).astype(BF16)
        a['akv'] = jnp.dot(a.pop('ak'), a['vbd'], preferred_element_type=F32)
        a['p'] = a['p'].astype(BF16)
        a.pop('n')

    st = {ch: st_ref[ci] for ci, ch in enumerate(chains)}
    for j in range(chunks):
        x = {}
        for ch in chains:
            x[ch] = _dot_nt(A[ch + (j,)]['ar'], st[ch])
        u = {}
        for ch in chains:
            a = A[ch + (j,)]
            u[ch] = _dot(a['p'], bdize(x[ch][:T] + a['akv']))
        for ch in chains:
            d, q = ch
            a = A[ch + (j,)]
            y = x[ch][T:] + jnp.dot(a['rbrk'], jnp.concatenate([bdize(u[ch]).astype(BF16), a['vbd']], axis=0),
                                    preferred_element_type=F32)
            cj = order[d][j]
            refs[d][6][cj * T:(cj + 1) * T, q * LANES:(q + 1) * LANES] = y.astype(yf_ref.dtype)
            uv = jnp.concatenate([u[ch], tile(refs[d][1], d, q, j)], axis=0)
            st[ch] = (st[ch] * a['gl'] + _dot(uv.T, a['bk'])) * bd
    for ci, ch in enumerate(chains):
        st_ref[ci] = st[ch]


def _rwkv_scan(r, v, kk, lw, b2, k2, lc, chunks, pairs):
    B, L, C = r.shape
    tb = chunks * RW_CHUNK
    wl = pairs * LANES
    assert L % tb == 0 and lc % tb == 0 and C % wl == 0
    nblk, nctx = L // tb, lc // tb

    def bwd(i):
        return jnp.where(i < nctx, nctx - 1 - i, nblk - 1 - (i - nctx))

    sf = pl.BlockSpec((None, tb, wl), lambda b, p, i: (b, i, p))
    sb = pl.BlockSpec((None, tb, wl), lambda b, p, i: (b, bwd(i), p))
    df = pl.BlockSpec((None, None, tb, wl), lambda b, p, i: (0, b, i, p))
    db = pl.BlockSpec((None, None, tb, wl), lambda b, p, i: (1, b, bwd(i), p))
    masks = _rwkv_masks()
    return pl.pallas_call(
        functools.partial(_rwkv_scan_kernel, chunks=chunks, pairs=pairs),
        grid=(B, C // wl, nblk),
        in_specs=[pl.BlockSpec(masks.shape, lambda b, p, i: (0, 0, 0, 0)),
                  sf, sf, sf, sb, sb, sb, df, df, df, db, db, db],
        out_specs=[sf, sb],
        out_shape=[jax.ShapeDtypeStruct((B, L, C), BF16)] * 2,
        scratch_shapes=[pltpu.VMEM((2 * pairs, LANES, LANES), F32)],
        compiler_params=pltpu.CompilerParams(
            dimension_semantics=("parallel", "parallel", "arbitrary"),
            vmem_limit_bytes=VMEM_LIMIT),
        name="rwkv_scan",
    )(masks, r, v, kk, r, v, kk, lw, b2, k2, lw, b2, k2)


def _ssm_conv_kernel(x_ref, xp_ref, xn_ref, w_ref, b_ref, o_ref, *, tt, lc, L):
    has_prev, has_next = _seg_edges(pl.program_id(1), tt, lc, L)
    ext = jnp.concatenate([xp_ref[...].astype(F32) * has_prev, x_ref[...].astype(F32),
                           xn_ref[...].astype(F32) * has_next], axis=0)
    n = tt + 2 * HALO
    acc = jnp.zeros(x_ref.shape, F32) + b_ref[...]
    for j in range(SSM_TAPS):
        off = j - SSM_TAPS // 2
        sh = ext if off == 0 else pltpu.roll(ext, (-off) % n, 0)
        acc = acc + sh[HALO:HALO + tt] * w_ref[j:j + 1, :]
    o_ref[...] = (acc * _sigmoid(acc)).astype(o_ref.dtype)


def _ssm_conv(zx, w, b, lc, tt, tw, col0):
    B, L, _ = zx.shape
    cc = w.shape[0]
    assert cc % tw == 0 and col0 % tw == 0 and L % tt == 0 and lc % tt == 0
    main, prev, nxt = _halo_specs(tt, tw, lambda c: c + col0 // tw, L)
    wt = jnp.transpose(w)
    return pl.pallas_call(
        functools.partial(_ssm_conv_kernel, tt=tt, lc=lc, L=L),
        grid=(B, L // tt, cc // tw),
        in_specs=[main, prev, nxt, pl.BlockSpec((SSM_TAPS, tw), lambda b, i, c: (0, c)),
                  pl.BlockSpec((1, tw), lambda b, i, c: (0, c))],
        out_specs=pl.BlockSpec((None, tt, tw), lambda b, i, c: (b, i, c)),
        out_shape=jax.ShapeDtypeStruct((B, L, cc), BF16),
        compiler_params=pltpu.CompilerParams(dimension_semantics=("parallel", "parallel", "parallel"),
                                             vmem_limit_bytes=VMEM_LIMIT),
        name="ssm_conv",
    )(zx, zx, zx, wt, b.reshape(1, cc))


def _ssd_scan_kernel(x_ref, bm_ref, cm_ref, dt_ref, dtt_ref, a_ref, at_ref, bias_ref, biast_ref, y_ref,
                     ht_ref, *, groups, hpg, sub):
    Q = SSD_CHUNK
    rev = pl.program_id(1) == 1

    @pl.when(pl.program_id(2) == 0)
    def _():
        ht_ref[...] = jnp.zeros_like(ht_ref)

    row = lax.broadcasted_iota(jnp.int32, (Q, Q), 0)
    col = lax.broadcasted_iota(jnp.int32, (Q, Q), 1)
    incl = jnp.where(rev, col - row, row - col) >= 0
    tri = incl.astype(F32)
    trib = tri.astype(BF16)
    nt = lambda a_, b_: lax.dot_general(a_, b_, (((1,), (1,)), ((), ())), preferred_element_type=F32)
    head0 = lax.broadcasted_iota(jnp.int32, (1, LANES), 1) < HEAD
    h0f = head0.astype(F32)
    h1f = 1.0 - h0f
    for j in range(sub):
        cj = jnp.where(rev, sub - 1 - j, j)
        tok = pl.ds(pl.multiple_of(cj * Q, Q), Q)
        dt = _softplus(dt_ref[cj] + bias_ref[...])
        cum = _dot_exact_lhs(tri, dt * a_ref[...])
        dtt = _softplus(dtt_ref[cj] + biast_ref[...])
        hi, mid, lo = _split3(dtt * at_ref[...])
        cumt = nt(hi, trib) + nt(mid, trib) + nt(lo, trib)
        for g in range(groups):
            bm = bm_ref[tok, g * SSM_STATE:(g + 1) * SSM_STATE]
            cm = cm_ref[tok, g * SSM_STATE:(g + 1) * SSM_STATE]
            cb = _dot_nt(cm, bm)
            bmt = bm.astype(F32).T
            for pp in range(hpg // 2):
                e0 = g * hpg + 2 * pp
                lanes = slice((e0 // 2) * LANES, (e0 // 2 + 1) * LANES)
                cum_l = jnp.where(head0, cum[:, e0:e0 + 1], cum[:, e0 + 1:e0 + 2])
                dt_l = jnp.where(head0, dt[:, e0:e0 + 1], dt[:, e0 + 1:e0 + 2])
                ms = []
                for e in (e0, e0 + 1):
                    seg = jnp.exp(jnp.where(incl, cum[:, e:e + 1] - cumt[e:e + 1, :], -jnp.inf))
                    ms.append(cb * seg)
                xdt = x_ref[tok, lanes].astype(F32) * dt_l
                clast = jnp.min(cum_l, axis=0, keepdims=True)
                htp = ht_ref[:, lanes]
                y = _dot(jnp.concatenate(ms, axis=1), jnp.concatenate([xdt * h0f, xdt * h1f], axis=0))
                y_ref[tok, lanes] = (y + _dot(cm, htp) * jnp.exp(cum_l)).astype(y_ref.dtype)
                ht_ref[:, lanes] = htp * jnp.exp(clast) + _dot(bmt, xdt * jnp.exp(clast - cum_l))


def _ssd_scan(xbc, dtp, p, lc, sub):
    B, L, _ = xbc.shape
    Q = SSD_CHUNK
    H, G, N = SSM_HEADS, SSM_GROUPS, SSM_STATE
    xw = H * HEAD
    gw = G * N
    tb = sub * Q
    assert L % tb == 0 and lc % tb == 0 and xw % gw == 0
    nblk, nctx = L // tb, lc // tb
    dt = jnp.pad(jnp.moveaxis(dtp.reshape(B, L, 2, H), 2, 1), ((0, 0), (0, 0), (0, 0), (0, LANES - H)))
    dt = dt.reshape(B, 2, L // Q, Q, LANES)
    dtt = jnp.swapaxes(dt, 3, 4)
    a = jnp.pad(-jnp.exp(p['ssm_a_log']), ((0, 0), (0, LANES - H)))
    bias = jnp.pad(p['ssm_dt_bias'], ((0, 0), (0, LANES - H)))

    def ck(d, i):
        bwd = jnp.where(i < nctx, nctx - 1 - i, nblk - 1 - (i - nctx))
        return jnp.where(d == 0, i, bwd)

    return pl.pallas_call(
        functools.partial(_ssd_scan_kernel, groups=G, hpg=H // G, sub=sub),
        grid=(B, 2, nblk),
        in_specs=[pl.BlockSpec((None, tb, xw), lambda b, d, i: (b, ck(d, i), 0)),
                  pl.BlockSpec((None, tb, gw), lambda b, d, i: (b, ck(d, i), xw // gw)),
                  pl.BlockSpec((None, tb, gw), lambda b, d, i: (b, ck(d, i), xw // gw + 1)),
                  pl.BlockSpec((None, None, sub, Q, LANES), lambda b, d, i: (b, d, ck(d, i), 0, 0)),
                  pl.BlockSpec((None, None, sub, LANES, Q), lambda b, d, i: (b, d, ck(d, i), 0, 0)),
                  pl.BlockSpec((None, 1, LANES), lambda b, d, i: (d, 0, 0)),
                  pl.BlockSpec((None, LANES, 1), lambda b, d, i: (d, 0, 0)),
                  pl.BlockSpec((None, 1, LANES), lambda b, d, i: (d, 0, 0)),
                  pl.BlockSpec((None, LANES, 1), lambda b, d, i: (d, 0, 0))],
        out_specs=pl.BlockSpec((None, None, tb, xw), lambda b, d, i: (d, b, ck(d, i), 0)),
        out_shape=jax.ShapeDtypeStruct((2, B, L, xw), BF16),
        scratch_shapes=[pltpu.VMEM((N, xw), F32)],
        compiler_params=pltpu.CompilerParams(
            dimension_semantics=("parallel", "arbitrary", "arbitrary"),
            vmem_limit_bytes=VMEM_LIMIT),
        name="ssd_scan",
    )(xbc, xbc, xbc, dt, dtt, a.reshape(2, 1, LANES), a.reshape(2, LANES, 1), bias.reshape(2, 1, LANES),
      bias.reshape(2, LANES, 1))


def _ssm_readout_kernel(yf_ref, yb_ref, x_ref, z_ref, d_ref, nw_ref, o_ref):
    z = z_ref[...].astype(F32)
    y = (yf_ref[...].astype(F32) + yb_ref[...].astype(F32) + d_ref[...] * x_ref[...].astype(F32)) * (z * _sigmoid(z))
    o_ref[...] = (y * lax.rsqrt(jnp.mean(y * y, axis=-1, keepdims=True) + NORM_EPS) * nw_ref[...]).astype(o_ref.dtype)


def _ssm_readout(y2, xbc, zx, p, lc, tt):
    _, B, L, C = y2.shape
    S = L - lc
    off = lc // tt
    d_l = jnp.repeat(p['ssm_d'], HEAD).reshape(1, C)
    row = lambda b, i: (b, i + off, 0)
    return pl.pallas_call(
        _ssm_readout_kernel,
        grid=(B, S // tt),
        in_specs=[pl.BlockSpec((None, None, tt, C), lambda b, i: (0, b, i + off, 0)),
                  pl.BlockSpec((None, None, tt, C), lambda b, i: (1, b, i + off, 0)),
                  pl.BlockSpec((None, tt, C), row), pl.BlockSpec((None, tt, C), row),
                  pl.BlockSpec((1, C), lambda b, i: (0, 0)), pl.BlockSpec((1, C), lambda b, i: (0, 0))],
        out_specs=pl.BlockSpec((None, tt, C), lambda b, i: (b, i, 0)),
        out_shape=jax.ShapeDtypeStruct((B, S, C), BF16),
        compiler_params=pltpu.CompilerParams(dimension_semantics=("parallel", "parallel"),
                                             vmem_limit_bytes=VMEM_LIMIT),
        name="ssm_readout",
    )(y2, y2, xbc, zx, d_l, p['ssm_norm'].reshape(1, C))


def _merge_kernel(yf_ref, yb_ref, bv_ref, g_ref, ys_ref, gt_ref, x_ref, mod_ref, lnw_ref, lnb_ref, n1_ref, n2_ref,
                  pa_ref, pb_ref, wo_ref, rt_ref, ones_ref, x1_ref, h2_ref, aff_ref):
    C = yf_ref.shape[-1]
    ones_bd = ones_ref[...]
    y = yf_ref[...].astype(F32) + yb_ref[...].astype(F32)
    mean = _head_sum(y, ones_bd) * (1.0 / HEAD)
    yc = y - mean
    var = _head_sum(yc * yc, ones_bd) * (1.0 / HEAD)
    ya = (yc * lax.rsqrt(var + RW_GN_EPS) * lnw_ref[...] + lnb_ref[...] + bv_ref[...].astype(F32)) * g_ref[...].astype(F32)
    gt = gt_ref[...].astype(F32)
    mix = _sigmoid(gt[:, :C]) * _dot(ya, pa_ref[...]) + _sigmoid(gt[:, C:]) * jnp.dot(
        ys_ref[...], pb_ref[...], preferred_element_type=F32)
    ml = _dot(mix, wo_ref[...])
    rms = lambda t: t * lax.rsqrt(jnp.mean(t * t, axis=-1, keepdims=True) + NORM_EPS)
    x1 = x_ref[...] + mod_ref[2:3, :] * (rms(ml) * n1_ref[...])
    x1_ref[...] = x1
    h2 = (rms(x1) * n2_ref[...] * (1.0 + mod_ref[4:5, :]) + mod_ref[3:4, :]).astype(BF16)
    h2_ref[...] = h2
    logits = lax.dot_general(rt_ref[...], h2, (((1,), (1,)), ((), ())), preferred_element_type=F32)
    e = jnp.exp(logits - jnp.max(logits, axis=0, keepdims=True))
    aff_ref[...] = e / jnp.sum(e, axis=0, keepdims=True)


def _merge(yf, yb, bv, g, yssm, gates, x, mods, p, n1post, n2pre, router, lc, tm):
    B, S, D = x.shape
    C = yf.shape[-1]
    E = router.shape[1]
    off = lc // tm
    lat = pl.BlockSpec((None, tm, C), lambda b, i: (b, i + off, 0))
    tok = lambda w: pl.BlockSpec((None, tm, w), lambda b, i: (b, i, 0))
    full = lambda a: pl.BlockSpec(a.shape, lambda b, i: (0,) * a.ndim, pipeline_mode=pl.Buffered(1))
    row = lambda a: a.reshape(1, -1)
    params = [row(p['rw_ln_w']), row(p['rw_ln_b']), row(n1post), row(n2pre), p['proj_a'].astype(BF16),
              p['proj_b'].astype(BF16), p['w_out'].astype(BF16), jnp.transpose(router).astype(BF16), _head_ones()]
    return pl.pallas_call(
        _merge_kernel,
        grid=(B, S // tm),
        in_specs=[lat, lat, lat, lat, tok(yssm.shape[-1]), tok(2 * C), tok(D),
                  pl.BlockSpec((None, 6, D), lambda b, i: (b, 0, 0))] + [full(a) for a in params],
        out_specs=[tok(D), tok(D), pl.BlockSpec((None, E, tm), lambda b, i: (b, 0, i))],
        out_shape=[jax.ShapeDtypeStruct((B, S, D), F32), jax.ShapeDtypeStruct((B, S, D), BF16),
                   jax.ShapeDtypeStruct((B, E, S), F32)],
        compiler_params=pltpu.CompilerParams(dimension_semantics=("parallel", "parallel"),
                                             vmem_limit_bytes=VMEM_LIMIT),
        name="merge",
    )(yf, yb, bv, g, yssm, gates, x, mods, *params)


def _select_kernel(aff_ref, slot_ref, off_ref, *, cap):
    aff = aff_ref[...]
    E, S = aff.shape
    bits = lax.bitcast_convert_type(aff, jnp.int32)
    count = lambda m: jnp.sum(m.astype(F32), axis=1, keepdims=True)
    thr = jnp.zeros((E, 1), jnp.int32)
    for bit in range(30, -1, -1):
        cand = thr | (1 << bit)
        thr = jnp.where(count(bits >= cand) >= cap, cand, thr)
    gt = bits > thr
    eq = bits == thr
    need = cap - count(gt)

    tl = lax.broadcasted_iota(jnp.int32, (LANES, LANES), 0)
    tc = lax.broadcasted_iota(jnp.int32, (LANES, LANES), 1)
    before = (tl < tc).astype(BF16)

    def prefix(m):
        outs, carry = [], jnp.zeros((E, 1), F32)
        for t in range(S // LANES):
            mt = m[:, t * LANES:(t + 1) * LANES]
            outs.append(jnp.dot(mt.astype(BF16), before, preferred_element_type=F32) + carry)
            carry = carry + jnp.sum(mt, axis=1, keepdims=True)
        return jnp.concatenate(outs, axis=1)

    eqf = eq.astype(F32)
    taken = gt.astype(F32) + eqf * (prefix(eqf) < need).astype(F32)
    slot_ref[...] = jnp.where(taken > 0.5, prefix(taken), -1.0).astype(jnp.int32)
    tok = lax.broadcasted_iota(jnp.int32, (S, LANES), 0)
    tile_start = lax.broadcasted_iota(jnp.int32, (S, LANES), 1) * LANES
    off_ref[...] = jnp.dot(taken.astype(BF16), (tok < tile_start).astype(BF16),
                           preferred_element_type=F32).astype(jnp.int32)


def _select(aff, cap):
    B, E, S = aff.shape
    spec = pl.BlockSpec((None, E, S), lambda b: (b, 0, 0))
    return pl.pallas_call(
        functools.partial(_select_kernel, cap=cap),
        grid=(B,), in_specs=[spec], out_specs=[spec, pl.BlockSpec((None, E, LANES), lambda b: (b, 0, 0))],
        out_shape=[jax.ShapeDtypeStruct((B, E, S), jnp.int32), jax.ShapeDtypeStruct((B, E, LANES), jnp.int32)],
        compiler_params=pltpu.CompilerParams(dimension_semantics=("parallel",), vmem_limit_bytes=VMEM_LIMIT),
        name="ec_select",
    )(aff)


def _moe_kernel(off_ref, h_ref, slot_ref, aff_ref, w1_ref, w3_ref, w2_ref, o_ref, xe_ref, gate_ref, ye_ref,
                *, cap, ts, sb):
    b, e = pl.program_id(0), pl.program_id(1)
    S, D = h_ref.shape
    nt, nsb = S // ts, cap // sb
    base = (b * pl.num_programs(1) + e) * (nt + 1)
    slot_iota = lax.broadcasted_iota(jnp.int32, (sb, ts), 0)

    def pairs(body):
        for t in range(nt):
            lo, hi = off_ref[base + t], off_ref[base + t + 1]
            for k in range(nsb):
                def run(t=t, k=k):
                    tok = slice(t * ts, (t + 1) * ts)
                    hit = (slot_iota + k * sb) == slot_ref[pl.ds(e, 1), tok]
                    body(hit, tok, slice(k * sb, (k + 1) * sb))
                pl.when(jnp.logical_and(lo < (k + 1) * sb, hi > k * sb))(run)

    xe_ref[...] = jnp.zeros_like(xe_ref)
    gate_ref[...] = jnp.zeros_like(gate_ref)

    @pl.when(e == 0)
    def _():
        o_ref[...] = jnp.zeros_like(o_ref)

    def gather(hit, tok, rows):
        gate_ref[rows, :] += jnp.sum(jnp.where(hit, aff_ref[pl.ds(e, 1), tok], 0.0), axis=1, keepdims=True)
        xe_ref[rows, :] += jnp.dot(jnp.where(hit, 1.0, 0.0).astype(BF16), h_ref[tok, :],
                                   preferred_element_type=F32)

    pairs(gather)
    xe = xe_ref[...].astype(BF16)
    a1 = jnp.dot(xe, w1_ref[...], preferred_element_type=F32)
    a3 = jnp.dot(xe, w3_ref[...], preferred_element_type=F32)
    hid = (a1 * _sigmoid(a1)) * a3
    ye_ref[...] = (_dot(hid, w2_ref[...]) * gate_ref[...]).astype(BF16)

    def scatter(hit, tok, rows):
        o_ref[tok, :] += lax.dot_general(jnp.where(hit, 1.0, 0.0).astype(BF16), ye_ref[rows, :],
                                         (((0,), (0,)), ((), ())), preferred_element_type=F32)

    pairs(scatter)


def _moe(h2, slot, offs, aff, w1, w3, w2, cap):
    B, S, D = h2.shape
    E = slot.shape[1]
    F = w1.shape[-1]
    ts, sb = min(S, 1024), min(cap, 2 * LANES)
    nt = S // ts
    bounds = jnp.concatenate([offs[..., ::ts // LANES][..., :nt], jnp.full((B, E, 1), cap, jnp.int32)], -1).reshape(-1)
    once = pl.Buffered(1)
    return pl.pallas_call(
        functools.partial(_moe_kernel, cap=cap, ts=ts, sb=sb),
        grid_spec=pltpu.PrefetchScalarGridSpec(
            num_scalar_prefetch=1,
            grid=(B, E),
            in_specs=[pl.BlockSpec((None, S, D), lambda b, e, o: (b, 0, 0), pipeline_mode=once),
                      pl.BlockSpec((None, E, S), lambda b, e, o: (b, 0, 0), pipeline_mode=once),
                      pl.BlockSpec((None, E, S), lambda b, e, o: (b, 0, 0), pipeline_mode=once),
                      pl.BlockSpec((None, D, F), lambda b, e, o: (e, 0, 0)),
                      pl.BlockSpec((None, D, F), lambda b, e, o: (e, 0, 0)),
                      pl.BlockSpec((None, F, D), lambda b, e, o: (e, 0, 0))],
            out_specs=pl.BlockSpec((None, S, D), lambda b, e, o: (b, 0, 0), pipeline_mode=once),
            scratch_shapes=[pltpu.VMEM((cap, D), F32), pltpu.VMEM((cap, 1), F32), pltpu.VMEM((cap, D), BF16)]),
        out_shape=jax.ShapeDtypeStruct((B, S, D), F32),
        compiler_params=pltpu.CompilerParams(dimension_semantics=("parallel", "arbitrary"),
                                             vmem_limit_bytes=MOE_VMEM_LIMIT),
        name="ec_moe",
    )(bounds, h2, slot, aff, w1, w3, w2)


def _final_kernel(x_ref, m_ref, g_ref, nw_ref, o_ref):
    m = m_ref[...]
    o_ref[...] = x_ref[...] + g_ref[...] * (m * lax.rsqrt(jnp.mean(m * m, axis=-1, keepdims=True) + NORM_EPS)
                                            * nw_ref[...])


def _final(x1, moe, g2, nw, tm):
    B, S, D = x1.shape
    tok = pl.BlockSpec((None, tm, D), lambda b, i: (b, i, 0))
    return pl.pallas_call(
        _final_kernel,
        grid=(B, S // tm),
        in_specs=[tok, tok, pl.BlockSpec((None, 1, D), lambda b, i: (b, 0, 0)), pl.BlockSpec((1, D), lambda b, i: (0, 0))],
        out_specs=tok,
        out_shape=jax.ShapeDtypeStruct((B, S, D), F32),
        compiler_params=pltpu.CompilerParams(dimension_semantics=("parallel", "parallel"),
                                             vmem_limit_bytes=VMEM_LIMIT),
        name="final_norm",
    )(x1, moe, g2, nw.reshape(1, D))


def _to_cm(t, rows):
    b, n, ch = t.shape
    return t.reshape(b, rows, GRID_W, ch).swapaxes(1, 2).reshape(b, n, ch)


def _from_cm(t, rows):
    b, n, ch = t.shape
    return t.reshape(b, GRID_W, rows, ch).swapaxes(1, 2).reshape(b, n, ch)


def kernel(x, c, ctx, c_ctx, ada_w, ada_b, norm1_pre, norm1_post, norm2_pre, norm2_post,
           w_in, rw_mu, rw_w0, rw_w2, rw_a0, rw_a2, rw_g2, rw_kk, rw_ka, rw_rk, rw_ln_w, rw_ln_b,
           ssm_conv_w, ssm_conv_b, ssm_dt_bias, ssm_a_log, ssm_d, ssm_norm, proj_a, proj_b, w_out,
           router, exp_w1, exp_w3, exp_w2):
    B, S, D = x.shape
    lc = ctx.shape[1]
    L = lc + S
    rows = S // GRID_W
    E = router.shape[-1]
    cap = 2 * S // E
    assert ada_w.shape[0] == 1, "single trunk layer"
    l = 0
    p = dict(rw_mu=rw_mu[l], rw_w0=rw_w0[l], rw_w2=rw_w2[l], rw_a0=rw_a0[l], rw_a2=rw_a2[l],
             rw_g2=rw_g2[l], rw_kk=rw_kk[l], rw_ka=rw_ka[l], rw_rk=rw_rk[l], rw_ln_w=rw_ln_w[l], rw_ln_b=rw_ln_b[l],
             ssm_conv_w=ssm_conv_w[l], ssm_conv_b=ssm_conv_b[l], ssm_dt_bias=ssm_dt_bias[l], ssm_a_log=ssm_a_log[l],
             ssm_d=ssm_d[l], ssm_norm=ssm_norm[l], proj_a=proj_a[l], proj_b=proj_b[l], w_out=w_out[l])
    big = lc % 256 == 0
    tt = 256 if big else 128
    chunks = 4 if big else 2
    n_rw = 3 * D + RW_LORA
    n_zx = 2 * D + (2 * D + 2 * SSM_GROUPS * SSM_STATE)
    n_dt = 2 * SSM_HEADS

    cond = jnp.concatenate([c, c_ctx[None], jnp.zeros((2 * HALO - B - 1, D), F32)], 0)
    mods_all = _adaln(cond, ada_w[l], ada_b[l], ada_w.shape[-1] // 4).reshape(2 * HALO, 6, D)
    mods, cmods = mods_all[:B], mods_all[B:B + 1]
    hl = _prenorm(x, mods, norm1_pre[l], tt)
    hc = _prenorm(ctx, cmods, norm1_pre[l], tt)
    seq_a = jnp.concatenate([hc, hl], 1).reshape(B * L, D)
    seq_b = jnp.concatenate([hc, _to_cm(hl, rows)], 1).reshape(B * L, D)
    w = w_in[l].astype(BF16)
    tm = 2048 if (B * L) % 2048 == 0 and (B * S) % 2048 == 0 else 128
    proj_rw = _matmul(seq_a, w[:, :n_rw], tm, n_rw // 3, BF16).reshape(B, L, n_rw)
    gates = _matmul(hl.reshape(B * S, D), w[:, n_rw + n_zx + n_dt:], tm, D, BF16).reshape(B, S, 2 * D)
    proj_z = _matmul(seq_b, w[:, n_rw:n_rw + 2 * D], tm, D, BF16).reshape(B, L, 2 * D)
    proj_xbc = _matmul(seq_b, w[:, n_rw + 2 * D:n_rw + n_zx], tm, D, BF16).reshape(B, L, n_zx - 2 * D)
    w_dt = jnp.pad(w[:, n_rw + n_zx:n_rw + n_zx + n_dt], ((0, 0), (0, LANES - n_dt)))
    proj_dt = _matmul(seq_b, w_dt, tm, LANES).reshape(B, L, LANES)[..., :n_dt]

    r, v, kk, lw, b2, ke, g, bv = _rwkv_prep(proj_rw, p, lc, tt)
    yf, ybw = _rwkv_scan(r, v, kk, lw, b2, ke, lc, chunks, 4)
    xbc = _ssm_conv(proj_xbc, p['ssm_conv_w'], p['ssm_conv_b'], lc, tt, proj_xbc.shape[-1], 0)
    y2 = _ssd_scan(xbc, proj_dt, p, lc, 2 if lc % (2 * SSD_CHUNK) == 0 else 1)
    yssm = _from_cm(_ssm_readout(y2, xbc, proj_z, p, lc, tt), rows)

    x1, h2, aff = _merge(yf, ybw, bv, g, yssm, gates, x, mods, p, norm1_post[l], norm2_pre[l], router[l], lc, tt)
    slot, offs = _select(aff, cap)
    moe = _moe(h2, slot, offs, aff, exp_w1[l].astype(BF16), exp_w3[l].astype(BF16), exp_w2[l].astype(BF16), cap)
    return _final(x1, moe, mods[:, 5:6], norm2_post[l], tt)
```

```python
import functools
import math

import jax
import jax.numpy as jnp
from jax import lax
from jax.experimental import pallas as pl
from jax.experimental.pallas import tpu as pltpu

F32 = jnp.float32
BF16 = jnp.bfloat16

GRID_W = 64
NORM_EPS = 1e-6
RW_GN_EPS = 64e-5
HEAD = 64
LANES = 128
HALO = 16
RW_CHUNK = 64
RW_LORA = 3 * LANES
SSM_TAPS = 5
SSD_CHUNK = 128
SSM_HEADS, SSM_GROUPS, SSM_STATE = 32, 4, 128
VMEM_LIMIT = 48 * 1024 * 1024
MOE_VMEM_LIMIT = 56 * 1024 * 1024


def _dot(a, b):
    return jnp.dot(a.astype(BF16), b.astype(BF16), preferred_element_type=F32)


def _dot_nt(a, b):
    return lax.dot_general(a.astype(BF16), b.astype(BF16), (((1,), (1,)), ((), ())), preferred_element_type=F32)


def _split3(x):
    hi = x.astype(BF16)
    r1 = x - hi.astype(F32)
    mid = r1.astype(BF16)
    lo = (r1 - mid.astype(F32)).astype(BF16)
    return hi, mid, lo


def _dot_exact_lhs(m, x, terms=3):
    mb = m.astype(BF16)
    return sum(jnp.dot(mb, t, preferred_element_type=F32) for t in _split3(x)[:terms])


def _sigmoid(x):
    return 1.0 / (1.0 + jnp.exp(-x))


def _softplus(x):
    return jnp.maximum(x, 0.0) + jnp.log(1.0 + jnp.exp(-jnp.abs(x)))


def _head_sum(x, ones_bd):
    hi = x.astype(BF16)
    lo = (x - hi.astype(F32)).astype(BF16)
    ob = ones_bd.astype(BF16)
    n = x.shape[1] // LANES
    part = lambda t, i: jnp.dot(t[:, i * LANES:(i + 1) * LANES], ob, preferred_element_type=F32)
    return jnp.concatenate([part(hi, i) + part(lo, i) for i in range(n)], axis=1)


def _head_ones():
    i = jnp.arange(LANES)
    return ((i[:, None] // HEAD) == (i[None, :] // HEAD)).astype(F32)


def _halo_specs(tt, width, col, L):
    per = tt // HALO
    nh = L // HALO
    main = pl.BlockSpec((None, tt, width), lambda b, i, *_: (b, i, col(*_)))
    prev = pl.BlockSpec((None, HALO, width), lambda b, i, *_: (b, jnp.maximum(i * per - 1, 0), col(*_)))
    nxt = pl.BlockSpec((None, HALO, width), lambda b, i, *_: (b, jnp.minimum((i + 1) * per, nh - 1), col(*_)))
    return main, prev, nxt


def _seg_edges(i, tt, lc, L):
    t0 = i * tt
    has_prev = jnp.logical_and(t0 != 0, t0 != lc)
    has_next = jnp.logical_and(t0 + tt != lc, t0 + tt != L)
    return has_prev.astype(F32), has_next.astype(F32)


def _adaln_kernel(c_ref, w_ref, b_ref, o_ref):
    c = c_ref[...]
    o_ref[...] = jnp.dot(c * _sigmoid(c), w_ref[...], preferred_element_type=F32,
                         precision=lax.Precision.HIGHEST) + b_ref[...]


def _adaln(cond, w, b, tn):
    M, D = cond.shape
    N = w.shape[1]
    return pl.pallas_call(
        _adaln_kernel,
        grid=(N // tn,),
        in_specs=[pl.BlockSpec((M, D), lambda j: (0, 0)), pl.BlockSpec((D, tn), lambda j: (0, j)),
                  pl.BlockSpec((1, tn), lambda j: (0, j))],
        out_specs=pl.BlockSpec((M, tn), lambda j: (0, j)),
        out_shape=jax.ShapeDtypeStruct((M, N), F32),
        compiler_params=pltpu.CompilerParams(dimension_semantics=("parallel",), vmem_limit_bytes=VMEM_LIMIT),
        name="adaln",
    )(cond, w, b.reshape(1, N))


def _prenorm_kernel(x_ref, mod_ref, nw_ref, o_ref):
    x = x_ref[...]
    y = x * lax.rsqrt(jnp.mean(x * x, axis=-1, keepdims=True) + NORM_EPS) * nw_ref[...]
    o_ref[...] = (y * (1.0 + mod_ref[1:2, :]) + mod_ref[0:1, :]).astype(o_ref.dtype)


def _prenorm(x, mods, nw, tm):
    B, N, D = x.shape
    per_sample = mods.shape[0] == B
    tok = pl.BlockSpec((None, tm, D), lambda b, i: (b, i, 0))
    return pl.pallas_call(
        _prenorm_kernel,
        grid=(B, N // tm),
        in_specs=[tok, pl.BlockSpec((None, 6, D), (lambda b, i: (b, 0, 0)) if per_sample else (lambda b, i: (0, 0, 0))),
                  pl.BlockSpec((1, D), lambda b, i: (0, 0))],
        out_specs=tok,
        out_shape=jax.ShapeDtypeStruct((B, N, D), BF16),
        compiler_params=pltpu.CompilerParams(dimension_semantics=("parallel", "parallel"),
                                             vmem_limit_bytes=VMEM_LIMIT),
        name="prenorm",
    )(x, mods, nw.reshape(1, D))


def _matmul_kernel(a_ref, w_ref, o_ref):
    o_ref[...] = jnp.dot(a_ref[...], w_ref[...], preferred_element_type=F32).astype(o_ref.dtype)


def _matmul(a, w, tm, tn, out_dtype=F32):
    M, K = a.shape
    N = w.shape[1]
    assert M % tm == 0 and N % tn == 0
    return pl.pallas_call(
        _matmul_kernel,
        grid=(N // tn, M // tm),
        in_specs=[pl.BlockSpec((tm, K), lambda j, i: (i, 0)), pl.BlockSpec((K, tn), lambda j, i: (0, j))],
        out_specs=pl.BlockSpec((tm, tn), lambda j, i: (i, j)),
        out_shape=jax.ShapeDtypeStruct((M, N), out_dtype),
        compiler_params=pltpu.CompilerParams(dimension_semantics=("parallel", "parallel"),
                                             vmem_limit_bytes=VMEM_LIMIT),
        name="matmul",
    )(a, w)


def _rwkv_prep_kernel(p_ref, pp_ref, pn_ref, mu_ref, w2_ref, a2_ref, g2_ref, w0_ref, a0_ref, kkw_ref, ka_ref,
                      rk_ref, ones_ref, r_ref, v_ref, kk_ref, lw_ref, b_ref, ke_ref, g_ref, bv_ref, *, tt, lc, L):
    has_prev, has_next = _seg_edges(pl.program_id(1), tt, lc, L)
    rows = lax.broadcasted_iota(jnp.int32, (tt, 1), 0)
    first, last = rows == 0, rows == tt - 1

    def shifted(lo, hi):
        u = p_ref[:, lo:hi].astype(F32)
        pv = pp_ref[HALO - 1:HALO, lo:hi].astype(F32) * has_prev
        nx = pn_ref[0:1, lo:hi].astype(F32) * has_next
        up = jnp.where(first, pv, pltpu.roll(u, 1, 0))
        un = jnp.where(last, nx, pltpu.roll(u, tt - 1, 0))
        return u + mu_ref[0:1, lo:hi] * (up - u) + mu_ref[1:2, lo:hi] * (un - u)

    C = r_ref.shape[-1]
    lane = lax.broadcasted_iota(jnp.int32, (1, LANES), 1)
    half = [(lane < HEAD).astype(F32), (lane >= HEAD).astype(F32)]
    xw = jnp.tanh(shifted(3 * C, 3 * C + LANES))
    xa = shifted(3 * C + LANES, 3 * C + 2 * LANES)
    xg = _sigmoid(shifted(3 * C + 2 * LANES, 3 * C + 3 * LANES))
    g_ref[...] = _dot(xg, g2_ref[...]).astype(g_ref.dtype)
    r = shifted(0, C)
    k = shifted(C, 2 * C)
    v = shifted(2 * C, 3 * C)
    r_ref[...] = r.astype(r_ref.dtype)
    v_ref[...] = v.astype(v_ref.dtype)
    ones_bd = ones_ref[...]
    kkr = k * kkw_ref[...]
    kk = kkr * lax.rsqrt(jnp.maximum(_head_sum(kkr * kkr, ones_bd), 1e-12))
    kk_ref[...] = kk.astype(kk_ref.dtype)
    ksum = jnp.zeros_like(k)
    for d in (0, 1):
        w_raw = w0_ref[d:d + 1, :] + _dot(xw * half[d], w2_ref[...])
        lw_ref[d] = -math.exp(-0.5) * _sigmoid(w_raw)
        a = _sigmoid(a0_ref[d:d + 1, :] + _dot(xa * half[d], a2_ref[...]))
        b_ref[d] = (kk * a).astype(b_ref.dtype)
        ke = k * (1.0 + (a - 1.0) * ka_ref[...])
        ke_ref[d] = ke.astype(ke_ref.dtype)
        ksum = ksum + ke
    bv_ref[...] = (_head_sum(r * ksum * rk_ref[...], ones_bd) * v).astype(bv_ref.dtype)


def _rwkv_prep(proj, p, lc, tt):
    B, L, W = proj.shape
    C = (W - RW_LORA) // 3
    assert L % tt == 0 and lc % tt == 0
    main, prev, nxt = _halo_specs(tt, W, lambda: 0, L)
    full = lambda a: pl.BlockSpec(a.shape, lambda b, i: (0,) * a.ndim)
    w2 = p['rw_w2'].reshape(2 * HEAD, C)
    a2 = p['rw_a2'].reshape(2 * HEAD, C)
    params = [p['rw_mu'], w2, a2, p['rw_g2'], p['rw_w0'], p['rw_a0'], p['rw_kk'].reshape(1, C),
              p['rw_ka'].reshape(1, C), p['rw_rk'].reshape(1, C), _head_ones()]
    one = pl.BlockSpec((None, tt, C), lambda b, i: (b, i, 0))
    two = pl.BlockSpec((2, None, tt, C), lambda b, i: (0, b, i, 0))
    s1 = jax.ShapeDtypeStruct((B, L, C), BF16)
    s2 = jax.ShapeDtypeStruct((2, B, L, C), BF16)
    return pl.pallas_call(
        functools.partial(_rwkv_prep_kernel, tt=tt, lc=lc, L=L),
        grid=(B, L // tt),
        in_specs=[main, prev, nxt] + [full(a) for a in params],
        out_specs=[one, one, one, two, two, two, one, one],
        out_shape=[s1, s1, s1, jax.ShapeDtypeStruct((2, B, L, C), F32), s2, s2, s1, s1],
        compiler_params=pltpu.CompilerParams(dimension_semantics=("parallel", "parallel"),
                                             vmem_limit_bytes=VMEM_LIMIT),
        name="rwkv_prep",
    )(proj, proj, proj, *params)


def _rwkv_masks():
    T = RW_CHUNK
    out = []
    for rev in (False, True):
        t = jnp.arange(T)
        t = (T - 1 - t) if rev else t
        rt, ct = t[:, None], jnp.tile(t, 2)[None, :]
        ms = [rt > ct, rt >= ct]
        s = 1
        while s < T:
            br, bc = rt // s, ct // s
            ms.append((br // 2 == bc // 2) & (br % 2 == 1) & (bc % 2 == 0))
            s *= 2
        out.append(jnp.stack(ms))
    return jnp.stack(out).astype(F32)


def _rwkv_scan_kernel(m_ref, rf_ref, vf_ref, kkf_ref, rb_ref, vb_ref, kkb_ref, lwf_ref, bf_ref, kf_ref,
                      lwb_ref, bb_ref, kb_ref, yf_ref, yb_ref, st_ref, *, chunks, pairs):
    T = RW_CHUNK

    @pl.when(pl.program_id(2) == 0)
    def _():
        st_ref[...] = jnp.zeros_like(st_ref)

    row = lax.broadcasted_iota(jnp.int32, (2 * T, 2 * T), 0)
    col = lax.broadcasted_iota(jnp.int32, (2 * T, 2 * T), 1)
    bd = ((row >> 6) == (col >> 6)).astype(F32)
    trow = lax.broadcasted_iota(jnp.int32, (T, LANES), 0)
    tcol = lax.broadcasted_iota(jnp.int32, (T, LANES), 1)
    eye = (trow == (tcol & (T - 1))).astype(F32)
    h0f = (tcol < HEAD).astype(F32)
    h1f = 1.0 - h0f
    bdize = lambda z: jnp.concatenate([z * h0f, z * h1f], axis=0)
    refs = ((rf_ref, vf_ref, kkf_ref, lwf_ref, bf_ref, kf_ref, yf_ref),
            (rb_ref, vb_ref, kkb_ref, lwb_ref, bb_ref, kb_ref, yb_ref))
    chains = [(d, q) for d in (0, 1) for q in range(pairs)]
    order = {0: list(range(chunks)), 1: list(range(chunks - 1, -1, -1))}
    units = [(d, q, j) for j in range(chunks) for (d, q) in chains]
    n_levels = RW_CHUNK.bit_length() - 1

    def tile(ref, d, q, j):
        cj = order[d][j]
        return ref[cj * T:(cj + 1) * T, q * LANES:(q + 1) * LANES].astype(F32)

    A = {}
    for un in units:
        d, q, j = un
        r_ref, v_ref, kk_ref, lw_ref, b_ref, k_ref, _ = refs[d]
        lw = tile(lw_ref, d, q, j)
        c = _dot_exact_lhs(m_ref[d, 1, :, :T], lw, terms=2)
        gl = jnp.exp(jnp.sum(lw, axis=0, keepdims=True))
        e_nc = jnp.exp(-c)
        at = -tile(kk_ref, d, q, j) * jnp.exp(c - lw)
        rtl = tile(r_ref, d, q, j) * jnp.exp(c)
        bt = tile(b_ref, d, q, j) * e_nc
        kt = tile(k_ref, d, q, j) * e_nc
        A[un] = dict(gl=gl, ar=jnp.concatenate([at, rtl], axis=0).astype(BF16),
                     bk=jnp.concatenate([bt * gl, kt * gl], axis=0).astype(BF16),
                     rhs=jnp.concatenate([bdize(bt), bdize(kt)], axis=0).astype(BF16))
    for un in units:
        a = A[un]
        a['aa'] = lax.dot_general(a['ar'], a.pop('rhs'), (((1,), (1,)), ((), ())),
                                  preferred_element_type=F32)
    for un in units:
        d = un[0]
        a = A[un]
        aa = a.pop('aa')
        a['n'] = aa[:T, :2 * T] * m_ref[d, 0]
        a['ak'] = (aa[:T, 2 * T:] * m_ref[d, 0]).astype(BF16)
        a['rbrk'] = jnp.concatenate([aa[T:, :2 * T] * m_ref[d, 1], aa[T:, 2 * T:] * m_ref[d, 1]], axis=1).astype(BF16)
        a['p'] = eye + a['n'] * m_ref[d, 2]
    for lv in range(1, n_levels):
        for un in units:
            a = A[un]
            a['pn'] = _dot(a['p'], bdize(a['n'] * m_ref[un[0], 2 + lv]))
        for un in units:
            a = A[un]
            a['p'] = a['p'] + _dot(a.pop('pn'), bdize(a['p']))
    for un in units:
        d, q, j = un
        a = A[un]
        a['vbd'] = bdize(tile(refs[d][1], d, q, j)).astype(BF16)
        a['akv'] = jnp.dot(a.pop('ak'), a['vbd'], preferred_element_type=F32)
        a['p'] = a['p'].astype(BF16)
        a.pop('n')

    st = {ch: st_ref[ci] for ci, ch in enumerate(chains)}
    for j in range(chunks):
        x = {}
        for ch in chains:
            x[ch] = _dot_nt(A[ch + (j,)]['ar'], st[ch])
        u = {}
        for ch in chains:
            a = A[ch + (j,)]
            u[ch] = _dot(a['p'], bdize(x[ch][:T] + a['akv']))
        for ch in chains:
            d, q = ch
            a = A[ch + (j,)]
            y = x[ch][T:] + jnp.dot(a['rbrk'], jnp.concatenate([bdize(u[ch]).astype(BF16), a['vbd']], axis=0),
                                    preferred_element_type=F32)
            cj = order[d][j]
            refs[d][6][cj * T:(cj + 1) * T, q * LANES:(q + 1) * LANES] = y.astype(yf_ref.dtype)
            uv = jnp.concatenate([u[ch], tile(refs[d][1], d, q, j)], axis=0)
            st[ch] = (st[ch] * a['gl'] + _dot(uv.T, a['bk'])) * bd
    for ci, ch in enumerate(chains):
        st_ref[ci] = st[ch]


def _rwkv_scan(r, v, kk, lw, b2, k2, lc, chunks, pairs):
    B, L, C = r.shape
    tb = chunks * RW_CHUNK
    wl = pairs * LANES
    assert L % tb == 0 and lc % tb == 0 and C % wl == 0
    nblk, nctx = L // tb, lc // tb

    def bwd(i):
        return jnp.where(i < nctx, nctx - 1 - i, nblk - 1 - (i - nctx))

    sf = pl.BlockSpec((None, tb, wl), lambda b, p, i: (b, i, p))
    sb = pl.BlockSpec((None, tb, wl), lambda b, p, i: (b, bwd(i), p))
    df = pl.BlockSpec((None, None, tb, wl), lambda b, p, i: (0, b, i, p))
    db = pl.BlockSpec((None, None, tb, wl), lambda b, p, i: (1, b, bwd(i), p))
    masks = _rwkv_masks()
    return pl.pallas_call(
        functools.partial(_rwkv_scan_kernel, chunks=chunks, pairs=pairs),
        grid=(B, C // wl, nblk),
        in_specs=[pl.BlockSpec(masks.shape, lambda b, p, i: (0, 0, 0, 0)),
                  sf, sf, sf, sb, sb, sb, df, df, df, db, db, db],
        out_specs=[sf, sb],
        out_shape=[jax.ShapeDtypeStruct((B, L, C), BF16)] * 2,
        scratch_shapes=[pltpu.VMEM((2 * pairs, LANES, LANES), F32)],
        compiler_params=pltpu.CompilerParams(
            dimension_semantics=("parallel", "parallel", "arbitrary"),
            vmem_limit_bytes=VMEM_LIMIT),
        name="rwkv_scan",
    )(masks, r, v, kk, r, v, kk, lw, b2, k2, lw, b2, k2)


def _ssm_conv_kernel(x_ref, xp_ref, xn_ref, w_ref, b_ref, o_ref, *, tt, lc, L):
    has_prev, has_next = _seg_edges(pl.program_id(1), tt, lc, L)
    ext = jnp.concatenate([xp_ref[...].astype(F32) * has_prev, x_ref[...].astype(F32),
                           xn_ref[...].astype(F32) * has_next], axis=0)
    n = tt + 2 * HALO
    acc = jnp.zeros(x_ref.shape, F32) + b_ref[...]
    for j in range(SSM_TAPS):
        off = j - SSM_TAPS // 2
        sh = ext if off == 0 else pltpu.roll(ext, (-off) % n, 0)
        acc = acc + sh[HALO:HALO + tt] * w_ref[j:j + 1, :]
    o_ref[...] = (acc * _sigmoid(acc)).astype(o_ref.dtype)


def _ssm_conv(zx, w, b, lc, tt, tw, col0):
    B, L, _ = zx.shape
    cc = w.shape[0]
    assert cc % tw == 0 and col0 % tw == 0 and L % tt == 0 and lc % tt == 0
    main, prev, nxt = _halo_specs(tt, tw, lambda c: c + col0 // tw, L)
    wt = jnp.transpose(w)
    return pl.pallas_call(
        functools.partial(_ssm_conv_kernel, tt=tt, lc=lc, L=L),
        grid=(B, L // tt, cc // tw),
        in_specs=[main, prev, nxt, pl.BlockSpec((SSM_TAPS, tw), lambda b, i, c: (0, c)),
                  pl.BlockSpec((1, tw), lambda b, i, c: (0, c))],
        out_specs=pl.BlockSpec((None, tt, tw), lambda b, i, c: (b, i, c)),
        out_shape=jax.ShapeDtypeStruct((B, L, cc), BF16),
        compiler_params=pltpu.CompilerParams(dimension_semantics=("parallel", "parallel", "parallel"),
                                             vmem_limit_bytes=VMEM_LIMIT),
        name="ssm_conv",
    )(zx, zx, zx, wt, b.reshape(1, cc))


def _ssd_scan_kernel(x_ref, bm_ref, cm_ref, dt_ref, dtt_ref, a_ref, at_ref, bias_ref, biast_ref, y_ref,
                     ht_ref, *, groups, hpg, sub):
    Q = SSD_CHUNK
    rev = pl.program_id(1) == 1

    @pl.when(pl.program_id(2) == 0)
    def _():
        ht_ref[...] = jnp.zeros_like(ht_ref)

    row = lax.broadcasted_iota(jnp.int32, (Q, Q), 0)
    col = lax.broadcasted_iota(jnp.int32, (Q, Q), 1)
    incl = jnp.where(rev, col - row, row - col) >= 0
    tri = incl.astype(F32)
    trib = tri.astype(BF16)
    nt = lambda a_, b_: lax.dot_general(a_, b_, (((1,), (1,)), ((), ())), preferred_element_type=F32)
    head0 = lax.broadcasted_iota(jnp.int32, (1, LANES), 1) < HEAD
    h0f = head0.astype(F32)
    h1f = 1.0 - h0f
    for j in range(sub):
        cj = jnp.where(rev, sub - 1 - j, j)
        tok = pl.ds(pl.multiple_of(cj * Q, Q), Q)
        dt = _softplus(dt_ref[cj] + bias_ref[...])
        cum = _dot_exact_lhs(tri, dt * a_ref[...])
        dtt = _softplus(dtt_ref[cj] + biast_ref[...])
        hi, mid, lo = _split3(dtt * at_ref[...])
        cumt = nt(hi, trib) + nt(mid, trib) + nt(lo, trib)
        for g in range(groups):
            bm = bm_ref[tok, g * SSM_STATE:(g + 1) * SSM_STATE]
            cm = cm_ref[tok, g * SSM_STATE:(g + 1) * SSM_STATE]
            cb = _dot_nt(cm, bm)
            bmt = bm.astype(F32).T
            for pp in range(hpg // 2):
                e0 = g * hpg + 2 * pp
                lanes = slice((e0 // 2) * LANES, (e0 // 2 + 1) * LANES)
                cum_l = jnp.where(head0, cum[:, e0:e0 + 1], cum[:, e0 + 1:e0 + 2])
                dt_l = jnp.where(head0, dt[:, e0:e0 + 1], dt[:, e0 + 1:e0 + 2])
                ms = []
                for e in (e0, e0 + 1):
                    seg = jnp.exp(jnp.where(incl, cum[:, e:e + 1] - cumt[e:e + 1, :], -jnp.inf))
                    ms.append(cb * seg)
                xdt = x_ref[tok, lanes].astype(F32) * dt_l
                clast = jnp.min(cum_l, axis=0, keepdims=True)
                htp = ht_ref[:, lanes]
                y = _dot(jnp.concatenate(ms, axis=1), jnp.concatenate([xdt * h0f, xdt * h1f], axis=0))
                y_ref[tok, lanes] = (y + _dot(cm, htp) * jnp.exp(cum_l)).astype(y_ref.dtype)
                ht_ref[:, lanes] = htp * jnp.exp(clast) + _dot(bmt, xdt * jnp.exp(clast - cum_l))


def _ssd_scan(xbc, dtp, p, lc, sub):
    B, L, _ = xbc.shape
    Q = SSD_CHUNK
    H, G, N = SSM_HEADS, SSM_GROUPS, SSM_STATE
    xw = H * HEAD
    gw = G * N
    tb = sub * Q
    assert L % tb == 0 and lc % tb == 0 and xw % gw == 0
    nblk, nctx = L // tb, lc // tb
    dt = jnp.pad(jnp.moveaxis(dtp.reshape(B, L, 2, H), 2, 1), ((0, 0), (0, 0), (0, 0), (0, LANES - H)))
    dt = dt.reshape(B, 2, L // Q, Q, LANES)
    dtt = jnp.swapaxes(dt, 3, 4)
    a = jnp.pad(-jnp.exp(p['ssm_a_log']), ((0, 0), (0, LANES - H)))
    bias = jnp.pad(p['ssm_dt_bias'], ((0, 0), (0, LANES - H)))

    def ck(d, i):
        bwd = jnp.where(i < nctx, nctx - 1 - i, nblk - 1 - (i - nctx))
        return jnp.where(d == 0, i, bwd)

    return pl.pallas_call(
        functools.partial(_ssd_scan_kernel, groups=G, hpg=H // G, sub=sub),
        grid=(B, 2, nblk),
        in_specs=[pl.BlockSpec((None, tb, xw), lambda b, d, i: (b, ck(d, i), 0)),
                  pl.BlockSpec((None, tb, gw), lambda b, d, i: (b, ck(d, i), xw // gw)),
                  pl.BlockSpec((None, tb, gw), lambda b, d, i: (b, ck(d, i), xw // gw + 1)),
                  pl.BlockSpec((None, None, sub, Q, LANES), lambda b, d, i: (b, d, ck(d, i), 0, 0)),
                  pl.BlockSpec((None, None, sub, LANES, Q), lambda b, d, i: (b, d, ck(d, i), 0, 0)),
                  pl.BlockSpec((None, 1, LANES), lambda b, d, i: (d, 0, 0)),
                  pl.BlockSpec((None, LANES, 1), lambda b, d, i: (d, 0, 0)),
                  pl.BlockSpec((None, 1, LANES), lambda b, d, i: (d, 0, 0)),
                  pl.BlockSpec((None, LANES, 1), lambda b, d, i: (d, 0, 0))],
        out_specs=pl.BlockSpec((None, None, tb, xw), lambda b, d, i: (d, b, ck(d, i), 0)),
        out_shape=jax.ShapeDtypeStruct((2, B, L, xw), BF16),
        scratch_shapes=[pltpu.VMEM((N, xw), F32)],
        compiler_params=pltpu.CompilerParams(
            dimension_semantics=("parallel", "arbitrary", "arbitrary"),
            vmem_limit_bytes=VMEM_LIMIT),
        name="ssd_scan",
    )(xbc, xbc, xbc, dt, dtt, a.reshape(2, 1, LANES), a.reshape(2, LANES, 1), bias.reshape(2, 1, LANES),
      bias.reshape(2, LANES, 1))


def _ssm_readout_kernel(yf_ref, yb_ref, x_ref, z_ref, d_ref, nw_ref, o_ref):
    z = z_ref[...].astype(F32)
    y = (yf_ref[...].astype(F32) + yb_ref[...].astype(F32) + d_ref[...] * x_ref[...].astype(F32)) * (z * _sigmoid(z))
    o_ref[...] = (y * lax.rsqrt(jnp.mean(y * y, axis=-1, keepdims=True) + NORM_EPS) * nw_ref[...]).astype(o_ref.dtype)


def _ssm_readout(y2, xbc, zx, p, lc, tt):
    _, B, L, C = y2.shape
    S = L - lc
    off = lc // tt
    d_l = jnp.repeat(p['ssm_d'], HEAD).reshape(1, C)
    row = lambda b, i: (b, i + off, 0)
    return pl.pallas_call(
        _ssm_readout_kernel,
        grid=(B, S // tt),
        in_specs=[pl.BlockSpec((None, None, tt, C), lambda b, i: (0, b, i + off, 0)),
                  pl.BlockSpec((None, None, tt, C), lambda b, i: (1, b, i + off, 0)),
                  pl.BlockSpec((None, tt, C), row), pl.BlockSpec((None, tt, C), row),
                  pl.BlockSpec((1, C), lambda b, i: (0, 0)), pl.BlockSpec((1, C), lambda b, i: (0, 0))],
        out_specs=pl.BlockSpec((None, tt, C), lambda b, i: (b, i, 0)),
        out_shape=jax.ShapeDtypeStruct((B, S, C), BF16),
        compiler_params=pltpu.CompilerParams(dimension_semantics=("parallel", "parallel"),
                                             vmem_limit_bytes=VMEM_LIMIT),
        name="ssm_readout",
    )(y2, y2, xbc, zx, d_l, p['ssm_norm'].reshape(1, C))


def _merge_kernel(yf_ref, yb_ref, bv_ref, g_ref, ys_ref, gt_ref, x_ref, mod_ref, lnw_ref, lnb_ref, n1_ref, n2_ref,
                  pa_ref, pb_ref, wo_ref, rt_ref, ones_ref, x1_ref, h2_ref, aff_ref):
    C = yf_ref.shape[-1]
    ones_bd = ones_ref[...]
    y = yf_ref[...].astype(F32) + yb_ref[...].astype(F32)
    mean = _head_sum(y, ones_bd) * (1.0 / HEAD)
    yc = y - mean
    var = _head_sum(yc * yc, ones_bd) * (1.0 / HEAD)
    ya = (yc * lax.rsqrt(var + RW_GN_EPS) * lnw_ref[...] + lnb_ref[...] + bv_ref[...].astype(F32)) * g_ref[...].astype(F32)
    gt = gt_ref[...].astype(F32)
    mix = _sigmoid(gt[:, :C]) * _dot(ya, pa_ref[...]) + _sigmoid(gt[:, C:]) * jnp.dot(
        ys_ref[...], pb_ref[...], preferred_element_type=F32)
    ml = _dot(mix, wo_ref[...])
    rms = lambda t: t * lax.rsqrt(jnp.mean(t * t, axis=-1, keepdims=True) + NORM_EPS)
    x1 = x_ref[...] + mod_ref[2:3, :] * (rms(ml) * n1_ref[...])
    x1_ref[...] = x1
    h2 = (rms(x1) * n2_ref[...] * (1.0 + mod_ref[4:5, :]) + mod_ref[3:4, :]).astype(BF16)
    h2_ref[...] = h2
    logits = lax.dot_general(rt_ref[...], h2, (((1,), (1,)), ((), ())), preferred_element_type=F32)
    e = jnp.exp(logits - jnp.max(logits, axis=0, keepdims=True))
    aff_ref[...] = e / jnp.sum(e, axis=0, keepdims=True)


def _merge(yf, yb, bv, g, yssm, gates, x, mods, p, n1post, n2pre, router, lc, tm):
    B, S, D = x.shape
    C = yf.shape[-1]
    E = router.shape[1]
    off = lc // tm
    lat = pl.BlockSpec((None, tm, C), lambda b, i: (b, i + off, 0))
    tok = lambda w: pl.BlockSpec((None, tm, w), lambda b, i: (b, i, 0))
    full = lambda a: pl.BlockSpec(a.shape, lambda b, i: (0,) * a.ndim, pipeline_mode=pl.Buffered(1))
    row = lambda a: a.reshape(1, -1)
    params = [row(p['rw_ln_w']), row(p['rw_ln_b']), row(n1post), row(n2pre), p['proj_a'].astype(BF16),
              p['proj_b'].astype(BF16), p['w_out'].astype(BF16), jnp.transpose(router).astype(BF16), _head_ones()]
    return pl.pallas_call(
        _merge_kernel,
        grid=(B, S // tm),
        in_specs=[lat, lat, lat, lat, tok(yssm.shape[-1]), tok(2 * C), tok(D),
                  pl.BlockSpec((None, 6, D), lambda b, i: (b, 0, 0))] + [full(a) for a in params],
        out_specs=[tok(D), tok(D), pl.BlockSpec((None, E, tm), lambda b, i: (b, 0, i))],
        out_shape=[jax.ShapeDtypeStruct((B, S, D), F32), jax.ShapeDtypeStruct((B, S, D), BF16),
                   jax.ShapeDtypeStruct((B, E, S), F32)],
        compiler_params=pltpu.CompilerParams(dimension_semantics=("parallel", "parallel"),
                                             vmem_limit_bytes=VMEM_LIMIT),
        name="merge",
    )(yf, yb, bv, g, yssm, gates, x, mods, *params)


def _select_kernel(aff_ref, slot_ref, off_ref, *, cap):
    aff = aff_ref[...]
    E, S = aff.shape
    bits = lax.bitcast_convert_type(aff, jnp.int32)
    count = lambda m: jnp.sum(m.astype(F32), axis=1, keepdims=True)
    thr = jnp.zeros((E, 1), jnp.int32)
    for bit in range(30, -1, -1):
        cand = thr | (1 << bit)
        thr = jnp.where(count(bits >= cand) >= cap, cand, thr)
    gt = bits > thr
    eq = bits == thr
    need = cap - count(gt)

    tl = lax.broadcasted_iota(jnp.int32, (LANES, LANES), 0)
    tc = lax.broadcasted_iota(jnp.int32, (LANES, LANES), 1)
    before = (tl < tc).astype(BF16)

    def prefix(m):
        outs, carry = [], jnp.zeros((E, 1), F32)
        for t in range(S // LANES):
            mt = m[:, t * LANES:(t + 1) * LANES]
            outs.append(jnp.dot(mt.astype(BF16), before, preferred_element_type=F32) + carry)
            carry = carry + jnp.sum(mt, axis=1, keepdims=True)
        return jnp.concatenate(outs, axis=1)

    eqf = eq.astype(F32)
    taken = gt.astype(F32) + eqf * (prefix(eqf) < need).astype(F32)
    slot_ref[...] = jnp.where(taken > 0.5, prefix(taken), -1.0).astype(jnp.int32)
    tok = lax.broadcasted_iota(jnp.int32, (S, LANES), 0)
    tile_start = lax.broadcasted_iota(jnp.int32, (S, LANES), 1) * LANES
    off_ref[...] = jnp.dot(taken.astype(BF16), (tok < tile_start).astype(BF16),
                           preferred_element_type=F32).astype(jnp.int32)


def _select(aff, cap):
    B, E, S = aff.shape
    spec = pl.BlockSpec((None, E, S), lambda b: (b, 0, 0))
    return pl.pallas_call(
        functools.partial(_select_kernel, cap=cap),
        grid=(B,), in_specs=[spec], out_specs=[spec, pl.BlockSpec((None, E, LANES), lambda b: (b, 0, 0))],
        out_shape=[jax.ShapeDtypeStruct((B, E, S), jnp.int32), jax.ShapeDtypeStruct((B, E, LANES), jnp.int32)],
        compiler_params=pltpu.CompilerParams(dimension_semantics=("parallel",), vmem_limit_bytes=VMEM_LIMIT),
        name="ec_select",
    )(aff)


def _moe_kernel(off_ref, h_ref, slot_ref, aff_ref, w1_ref, w3_ref, w2_ref, o_ref, xe_ref, gate_ref, ye_ref,
                *, cap, ts, sb):
    b, e = pl.program_id(0), pl.program_id(1)
    S, D = h_ref.shape
    nt, nsb = S // ts, cap // sb
    base = (b * pl.num_programs(1) + e) * (nt + 1)
    slot_iota = lax.broadcasted_iota(jnp.int32, (sb, ts), 0)

    def pairs(body):
        for t in range(nt):
            lo, hi = off_ref[base + t], off_ref[base + t + 1]
            for k in range(nsb):
                def run(t=t, k=k):
                    tok = slice(t * ts, (t + 1) * ts)
                    hit = (slot_iota + k * sb) == slot_ref[pl.ds(e, 1), tok]
                    body(hit, tok, slice(k * sb, (k + 1) * sb))
                pl.when(jnp.logical_and(lo < (k + 1) * sb, hi > k * sb))(run)

    xe_ref[...] = jnp.zeros_like(xe_ref)
    gate_ref[...] = jnp.zeros_like(gate_ref)

    @pl.when(e == 0)
    def _():
        o_ref[...] = jnp.zeros_like(o_ref)

    def gather(hit, tok, rows):
        gate_ref[rows, :] += jnp.sum(jnp.where(hit, aff_ref[pl.ds(e, 1), tok], 0.0), axis=1, keepdims=True)
        xe_ref[rows, :] += jnp.dot(jnp.where(hit, 1.0, 0.0).astype(BF16), h_ref[tok, :],
                                   preferred_element_type=F32)

    pairs(gather)
    xe = xe_ref[...].astype(BF16)
    a1 = jnp.dot(xe, w1_ref[...], preferred_element_type=F32)
    a3 = jnp.dot(xe, w3_ref[...], preferred_element_type=F32)
    hid = (a1 * _sigmoid(a1)) * a3
    ye_ref[...] = (_dot(hid, w2_ref[...]) * gate_ref[...]).astype(BF16)

    def scatter(hit, tok, rows):
        o_ref[tok, :] += lax.dot_general(jnp.where(hit, 1.0, 0.0).astype(BF16), ye_ref[rows, :],
                                         (((0,), (0,)), ((), ())), preferred_element_type=F32)

    pairs(scatter)


def _moe(h2, slot, offs, aff, w1, w3, w2, cap):
    B, S, D = h2.shape
    E = slot.shape[1]
    F = w1.shape[-1]
    ts, sb = min(S, 1024), min(cap, 2 * LANES)
    nt = S // ts
    bounds = jnp.concatenate([offs[..., ::ts // LANES][..., :nt], jnp.full((B, E, 1), cap, jnp.int32)], -1).reshape(-1)
    once = pl.Buffered(1)
    return pl.pallas_call(
        functools.partial(_moe_kernel, cap=cap, ts=ts, sb=sb),
        grid_spec=pltpu.PrefetchScalarGridSpec(
            num_scalar_prefetch=1,
            grid=(B, E),
            in_specs=[pl.BlockSpec((None, S, D), lambda b, e, o: (b, 0, 0), pipeline_mode=once),
                      pl.BlockSpec((None, E, S), lambda b, e, o: (b, 0, 0), pipeline_mode=once),
                      pl.BlockSpec((None, E, S), lambda b, e, o: (b, 0, 0), pipeline_mode=once),
                      pl.BlockSpec((None, D, F), lambda b, e, o: (e, 0, 0)),
                      pl.BlockSpec((None, D, F), lambda b, e, o: (e, 0, 0)),
                      pl.BlockSpec((None, F, D), lambda b, e, o: (e, 0, 0))],
            out_specs=pl.BlockSpec((None, S, D), lambda b, e, o: (b, 0, 0), pipeline_mode=once),
            scratch_shapes=[pltpu.VMEM((cap, D), F32), pltpu.VMEM((cap, 1), F32), pltpu.VMEM((cap, D), BF16)]),
        out_shape=jax.ShapeDtypeStruct((B, S, D), F32),
        compiler_params=pltpu.CompilerParams(dimension_semantics=("parallel", "arbitrary"),
                                             vmem_limit_bytes=MOE_VMEM_LIMIT),
        name="ec_moe",
    )(bounds, h2, slot, aff, w1, w3, w2)


def _final_kernel(x_ref, m_ref, g_ref, nw_ref, o_ref):
    m = m_ref[...]
    o_ref[...] = x_ref[...] + g_ref[...] * (m * lax.rsqrt(jnp.mean(m * m, axis=-1, keepdims=True) + NORM_EPS)
                                            * nw_ref[...])


def _final(x1, moe, g2, nw, tm):
    B, S, D = x1.shape
    tok = pl.BlockSpec((None, tm, D), lambda b, i: (b, i, 0))
    return pl.pallas_call(
        _final_kernel,
        grid=(B, S // tm),
        in_specs=[tok, tok, pl.BlockSpec((None, 1, D), lambda b, i: (b, 0, 0)), pl.BlockSpec((1, D), lambda b, i: (0, 0))],
        out_specs=tok,
        out_shape=jax.ShapeDtypeStruct((B, S, D), F32),
        compiler_params=pltpu.CompilerParams(dimension_semantics=("parallel", "parallel"),
                                             vmem_limit_bytes=VMEM_LIMIT),
        name="final_norm",
    )(x1, moe, g2, nw.reshape(1, D))


def _to_cm(t, rows):
    b, n, ch = t.shape
    return t.reshape(b, rows, GRID_W, ch).swapaxes(1, 2).reshape(b, n, ch)


def _from_cm(t, rows):
    b, n, ch = t.shape
    return t.reshape(b, GRID_W, rows, ch).swapaxes(1, 2).reshape(b, n, ch)


def kernel(x, c, ctx, c_ctx, ada_w, ada_b, norm1_pre, norm1_post, norm2_pre, norm2_post,
           w_in, rw_mu, rw_w0, rw_w2, rw_a0, rw_a2, rw_g2, rw_kk, rw_ka, rw_rk, rw_ln_w, rw_ln_b,
           ssm_conv_w, ssm_conv_b, ssm_dt_bias, ssm_a_log, ssm_d, ssm_norm, proj_a, proj_b, w_out,
           router, exp_w1, exp_w3, exp_w2):
    B, S, D = x.shape
    lc = ctx.shape[1]
    L = lc + S
    rows = S // GRID_W
    E = router.shape[-1]
    cap = 2 * S // E
    assert ada_w.shape[0] == 1, "single trunk layer"
    l = 0
    p = dict(rw_mu=rw_mu[l], rw_w0=rw_w0[l], rw_w2=rw_w2[l], rw_a0=rw_a0[l], rw_a2=rw_a2[l],
             rw_g2=rw_g2[l], rw_kk=rw_kk[l], rw_ka=rw_ka[l], rw_rk=rw_rk[l], rw_ln_w=rw_ln_w[l], rw_ln_b=rw_ln_b[l],
             ssm_conv_w=ssm_conv_w[l], ssm_conv_b=ssm_conv_b[l], ssm_dt_bias=ssm_dt_bias[l], ssm_a_log=ssm_a_log[l],
             ssm_d=ssm_d[l], ssm_norm=ssm_norm[l], proj_a=proj_a[l], proj_b=proj_b[l], w_out=w_out[l])
    big = lc % 256 == 0
    tt = 256 if big else 128
    chunks = 2
    n_rw = 3 * D + RW_LORA
    n_zx = 2 * D + (2 * D + 2 * SSM_GROUPS * SSM_STATE)
    n_dt = 2 * SSM_HEADS

    cond = jnp.concatenate([c, c_ctx[None], jnp.zeros((2 * HALO - B - 1, D), F32)], 0)
    mods_all = _adaln(cond, ada_w[l], ada_b[l], ada_w.shape[-1] // 4).reshape(2 * HALO, 6, D)
    mods, cmods = mods_all[:B], mods_all[B:B + 1]
    hl = _prenorm(x, mods, norm1_pre[l], tt)
    hc = _prenorm(ctx, cmods, norm1_pre[l], tt)
    seq_a = jnp.concatenate([hc, hl], 1).reshape(B * L, D)
    seq_b = jnp.concatenate([hc, _to_cm(hl, rows)], 1).reshape(B * L, D)
    w = w_in[l].astype(BF16)
    tm = 2048 if (B * L) % 2048 == 0 and (B * S) % 2048 == 0 else 128
    proj_rw = _matmul(seq_a, w[:, :n_rw], tm, n_rw // 3, BF16).reshape(B, L, n_rw)
    gates = _matmul(hl.reshape(B * S, D), w[:, n_rw + n_zx + n_dt:], tm, D, BF16).reshape(B, S, 2 * D)
    proj_z = _matmul(seq_b, w[:, n_rw:n_rw + 2 * D], tm, D, BF16).reshape(B, L, 2 * D)
    proj_xbc = _matmul(seq_b, w[:, n_rw + 2 * D:n_rw + n_zx], tm, D, BF16).reshape(B, L, n_zx - 2 * D)
    w_dt = jnp.pad(w[:, n_rw + n_zx:n_rw + n_zx + n_dt], ((0, 0), (0, LANES - n_dt)))
    proj_dt = _matmul(seq_b, w_dt, tm, LANES).reshape(B, L, LANES)[..., :n_dt]

    r, v, kk, lw, b2, ke, g, bv = _rwkv_prep(proj_rw, p, lc, tt)
    yf, ybw = _rwkv_scan(r, v, kk, lw, b2, ke, lc, chunks, D // LANES)
    xbc = _ssm_conv(proj_xbc, p['ssm_conv_w'], p['ssm_conv_b'], lc, tt, proj_xbc.shape[-1], 0)
    y2 = _ssd_scan(xbc, proj_dt, p, lc, 2 if lc % (2 * SSD_CHUNK) == 0 else 1)
    yssm = _from_cm(_ssm_readout(y2, xbc, proj_z, p, lc, tt), rows)

    x1, h2, aff = _merge(yf, ybw, bv, g, yssm, gates, x, mods, p, norm1_post[l], norm2_pre[l], router[l], lc, tt)
    slot, offs = _select(aff, cap)
    moe = _moe(h2, slot, offs, aff, exp_w1[l].astype(BF16), exp_w3[l].astype(BF16), exp_w2[l].astype(BF16), cap)
    return _final(x1, moe, mods[:, 5:6], norm2_post[l], tt)
```
